```python
import math
import jax, jax.numpy as jnp
from jax import lax
import numpy as np

D_MODEL = 1024
BATCH = 8
SEQ = 8192
DEPTH = 1
DEC_BATCH = 32
DEC_SEQ = 64
PAST_LEN = 1024

CHUNK = 64
N_META = 16
D_CONV = D_MODEL
CONV_WIDTH = 31
N_HEADS = 8
HEAD_DIM = 64
V_DIM = 2 * HEAD_DIM
D_ATTN = N_HEADS * V_DIM
Q_BLOCK = 128
NUM_BUCKETS = 32
MAX_DISTANCE = 128
PEER_HEADS = 8
PEER_TOPK = 16
N_KEYS = 128
N_EXPERTS = N_KEYS * N_KEYS
PEER_QDIM = 256
PEER_HALF = PEER_QDIM // 2
PEER_BLOCK = 128
ALPHA = (2.0 * DEPTH) ** 0.25
BETA = (8.0 * DEPTH) ** -0.25
LN_EPS = 1e-5
NEG_INF = -1e30
COL_GLU = 2 * D_CONV
COL_QK = N_HEADS * 2 * HEAD_DIM
COL_V = D_ATTN
COL_GATE = 2 * D_MODEL
D_IN = COL_GLU + 2 * COL_QK + COL_V + COL_GATE

kernel_name = "streaming_conformer_diffattn_peer"


def layer_norm(x, g, b):
    xf = x.astype(jnp.float32)
    mu = jnp.mean(xf, axis=-1, keepdims=True)
    var = jnp.mean(jnp.square(xf - mu), axis=-1, keepdims=True)
    return ((xf - mu) * lax.rsqrt(var + LN_EPS)).astype(x.dtype) * g + b


def rms_norm(x, g):
    xf = x.astype(jnp.float32)
    return (xf * lax.rsqrt(jnp.mean(xf * xf, axis=-1, keepdims=True) + LN_EPS)).astype(x.dtype) * g


def t5_bucket(rel):
    half = NUM_BUCKETS // 2
    max_exact = half // 2
    n = jnp.abs(rel)
    large = max_exact + (jnp.log(jnp.maximum(n, 1).astype(jnp.float32) / max_exact)
                         / math.log(MAX_DISTANCE / max_exact) * (half - max_exact)).astype(jnp.int32)
    large = jnp.minimum(large, half - 1)
    return jnp.where(rel > 0, half, 0) + jnp.where(n < max_exact, n, large)


def rel_bias_block(rel_bias, q_pos, k_pos):
    b = rel_bias[t5_bucket(k_pos[None, :] - q_pos[:, None])]
    return jnp.transpose(b, (2, 0, 1)).astype(jnp.float32)


def chunk_id(pos):
    return jnp.where(pos < N_META, -1, (pos - N_META) // CHUNK)


def diff_attention(q, k, v, bias, mask, lam):
    s = jnp.einsum('bqhmd,bkhmd->bhmqk', q, k).astype(jnp.float32) * (HEAD_DIM ** -0.5)
    s = s + bias[None, :, None]
    if mask is not None:
        s = jnp.where(mask, s, NEG_INF)
    p = jax.nn.softmax(s, axis=-1)
    w = p[:, :, 0] - lam * p[:, :, 1]
    return jnp.einsum('bhqk,bkhe->bqhe', w.astype(v.dtype), v)


def prompt_attention(q, k, v, rel_bias, lam):
    L = q.shape[1]
    pos = jnp.arange(L)
    cid = chunk_id(pos)
    outs = [diff_attention(q[:, :N_META], k[:, :N_META], v[:, :N_META],
                           rel_bias_block(rel_bias, pos[:N_META], pos[:N_META]), None, lam)]
    for j in range((L - N_META) // Q_BLOCK):
        q0 = N_META + j * Q_BLOCK
        q1 = q0 + Q_BLOCK
        mask = cid[None, :q1] <= cid[q0:q1, None]
        outs.append(diff_attention(q[:, q0:q1], k[:, :q1], v[:, :q1],
                                   rel_bias_block(rel_bias, pos[q0:q1], pos[:q1]), mask, lam))
    return jnp.concatenate(outs, axis=1)


def sample_attention(q, k_all, v_all, rel_bias, lam):
    n_new, n_all = q.shape[1], k_all.shape[1]
    k_pos = jnp.arange(n_all)
    q_pos = k_pos[n_all - n_new:]
    return diff_attention(q, k_all, v_all, rel_bias_block(rel_bias, q_pos, k_pos), None, lam)


def split_projection(h, w_in_l):
    z = h @ w_in_l
    return jnp.split(z, [COL_GLU, COL_GLU + COL_QK, COL_GLU + 2 * COL_QK,
                         COL_GLU + 2 * COL_QK + COL_V], axis=-1)


def conv_branch(glu, conv_state, conv_w, conv_b, ln_g, ln_b, w_conv_out):
    a, gate = jnp.split(glu, 2, axis=-1)
    u = a * jax.nn.sigmoid(gate)
    up = jnp.concatenate([conv_state.astype(u.dtype), u], axis=1)
    c = lax.conv_general_dilated(up, conv_w[:, None, :].astype(u.dtype), window_strides=(1,),
                                 padding='VALID', dimension_numbers=('NWC', 'WIO', 'NWC'),
                                 feature_group_count=D_CONV) + conv_b
    out = jax.nn.silu(layer_norm(c, ln_g, ln_b)) @ w_conv_out
    return out, up[:, up.shape[1] - (CONV_WIDTH - 1):]


def attn_branch_out(o, g, lam_init):
    return (rms_norm(o, g) * (1.0 - lam_init)).reshape(o.shape[0], o.shape[1], D_ATTN)


def merge_and_norm(h, conv_out, attn_out, gates, b_gate_l, w_out_l, ln_g, ln_b):
    g_conv, g_attn = jnp.split(jax.nn.sigmoid(gates + b_gate_l), 2, axis=-1)
    y = (g_conv * conv_out + g_attn * attn_out) @ w_out_l
    return layer_norm(ALPHA * h + y, ln_g, ln_b)


def peer(h, w_q, sub_keys, peer_u, peer_v):
    shape = h.shape
    x = h.reshape(-1, D_MODEL)
    n = x.shape[0]
    x = jnp.pad(x, ((0, (-n) % PEER_BLOCK), (0, 0)))

    def block(xb):
        q = (xb @ w_q).reshape(PEER_BLOCK, PEER_HEADS, 2, PEER_HALF)
        s = jnp.einsum('nhpd,pkd->nhpk', q, sub_keys).astype(jnp.float32)
        sv, si = lax.top_k(s, PEER_TOPK)
        cand = sv[:, :, 0, :, None] + sv[:, :, 1, None, :]
        cidx = si[:, :, 0, :, None] * N_KEYS + si[:, :, 1, None, :]
        top_s, top_j = lax.top_k(cand.reshape(PEER_BLOCK, PEER_HEADS, PEER_TOPK * PEER_TOPK), PEER_TOPK)
        experts = jnp.take_along_axis(cidx.reshape(PEER_BLOCK, PEER_HEADS, -1), top_j, axis=-1)
        g = jax.nn.softmax(top_s, axis=-1)
        act = jax.nn.gelu(jnp.einsum('nhkd,nd->nhk', peer_u[experts], xb).astype(jnp.float32),
                          approximate=False)
        return jnp.einsum('nhk,nhkd->nd', (g * act).astype(xb.dtype), peer_v[experts])

    y = lax.map(block, x.reshape(-1, PEER_BLOCK, D_MODEL))
    return y.reshape(-1, D_MODEL)[:n].reshape(shape)


def setup_inputs(seed: int = 0) -> dict:
    key = jax.random.key(seed)
    ks = jax.random.split(key, 32)
    f32 = jnp.float32

    def nrm(k, shape, scale):
        return jax.random.normal(k, shape, f32) * scale

    def gain(k, shape):
        return 1.0 + 0.02 * jax.random.normal(k, shape, f32)

    return {
        "x_prompt": nrm(ks[0], (BATCH, SEQ, D_MODEL), 1.0),
        "x_sample": nrm(ks[1], (DEC_BATCH, DEC_SEQ, D_MODEL), 1.0),
        "cache_k": nrm(ks[2], (DEPTH, DEC_BATCH, PAST_LEN, N_HEADS, 2 * HEAD_DIM), 1.0),
        "cache_v": nrm(ks[3], (DEPTH, DEC_BATCH, PAST_LEN, N_HEADS, V_DIM), 1.0),
        "state_conv": nrm(ks[4], (DEPTH, DEC_BATCH, CONV_WIDTH - 1, D_CONV), 1.0),
        "meta_tokens": nrm(ks[5], (N_META, D_MODEL), 1.0),
        "ln_in_g": gain(ks[6], (D_MODEL,)),
        "ln_in_b": nrm(ks[7], (D_MODEL,), 0.02),
        "rel_bias": nrm(ks[8], (NUM_BUCKETS, N_HEADS), 0.5),
        "w_in": nrm(ks[9], (DEPTH, D_MODEL, D_IN), D_MODEL ** -0.5),
        "b_gate": nrm(ks[10], (DEPTH, COL_GATE), 0.02),
        "conv_w": nrm(ks[11], (DEPTH, CONV_WIDTH, D_CONV), CONV_WIDTH ** -0.5),
        "conv_b": nrm(ks[12], (DEPTH, D_CONV), 0.02),
        "conv_ln_g": gain(ks[13], (DEPTH, D_CONV)),
        "conv_ln_b": nrm(ks[14], (DEPTH, D_CONV), 0.02),
        "w_conv_out": nrm(ks[15], (DEPTH, D_CONV, D_MODEL), BETA * D_CONV ** -0.5),
        "lam_q1": nrm(ks[16], (DEPTH, HEAD_DIM), 0.1),
        "lam_k1": nrm(ks[17], (DEPTH, HEAD_DIM), 0.1),
        "lam_q2": nrm(ks[18], (DEPTH, HEAD_DIM), 0.1),
        "lam_k2": nrm(ks[19], (DEPTH, HEAD_DIM), 0.1),
        "subln_g": gain(ks[20], (DEPTH, V_DIM)),
        "w_out": nrm(ks[21], (DEPTH, D_MODEL, D_MODEL), BETA * D_MODEL ** -0.5),
        "ln1_g": gain(ks[22], (DEPTH, D_MODEL)),
        "ln1_b": nrm(ks[23], (DEPTH, D_MODEL), 0.02),
        "w_peer_q": nrm(ks[24], (DEPTH, D_MODEL, PEER_HEADS * PEER_QDIM), D_MODEL ** -0.5),
        "sub_keys": nrm(ks[25], (DEPTH, 2, N_KEYS, PEER_HALF), PEER_HALF ** -0.5),
        "peer_u": nrm(ks[26], (DEPTH, N_EXPERTS, D_MODEL), D_MODEL ** -0.5),
        "peer_v": nrm(ks[27], (DEPTH, N_EXPERTS, D_MODEL), BETA),
        "ln2_g": gain(ks[28], (DEPTH, D_MODEL)),
        "ln2_b": nrm(ks[29], (DEPTH, D_MODEL), 0.02),
    }


def reference(x_prompt, x_sample, cache_k, cache_v, state_conv, meta_tokens, ln_in_g, ln_in_b,
              rel_bias, w_in, b_gate, conv_w, conv_b, conv_ln_g, conv_ln_b, w_conv_out,
              lam_q1, lam_k1, lam_q2, lam_k2, subln_g, w_out, ln1_g, ln1_b,
              w_peer_q, sub_keys, peer_u, peer_v, ln2_g, ln2_b):
    B, Bd = x_prompt.shape[0], x_sample.shape[0]
    meta = jnp.broadcast_to(meta_tokens.astype(x_prompt.dtype)[None], (B, N_META, D_MODEL))
    hp = layer_norm(jnp.concatenate([meta, x_prompt], axis=1), ln_in_g, ln_in_b)
    hs = layer_norm(x_sample, ln_in_g, ln_in_b)

    k_p_rows, v_p_rows, conv_p_rows, k_s_rows, v_s_rows, conv_s_rows = [], [], [], [], [], []
    for l in range(DEPTH):
        lam_init = 0.8 - 0.6 * math.exp(-0.3 * l)
        lam = (jnp.exp(jnp.sum(lam_q1[l] * lam_k1[l]).astype(jnp.float32))
               - jnp.exp(jnp.sum(lam_q2[l] * lam_k2[l]).astype(jnp.float32)) + lam_init)

        Lp = hp.shape[1]
        glu, q, k, v, gates = split_projection(hp, w_in[l])
        k_rows = k.reshape(B, Lp, N_HEADS, 2 * HEAD_DIM)
        v_rows = v.reshape(B, Lp, N_HEADS, V_DIM)
        conv_out, conv_st = conv_branch(glu, jnp.zeros((B, CONV_WIDTH - 1, D_CONV), glu.dtype),
                                        conv_w[l], conv_b[l], conv_ln_g[l], conv_ln_b[l], w_conv_out[l])
        o = prompt_attention(q.reshape(B, Lp, N_HEADS, 2, HEAD_DIM),
                             k_rows.reshape(B, Lp, N_HEADS, 2, HEAD_DIM), v_rows, rel_bias, lam)
        hp = merge_and_norm(hp, conv_out, attn_branch_out(o, subln_g[l], lam_init), gates,
                            b_gate[l], w_out[l], ln1_g[l], ln1_b[l])
        if l == DEPTH - 1:
            hp = hp[:, N_META:]
        hp = layer_norm(ALPHA * hp + peer(hp, w_peer_q[l], sub_keys[l], peer_u[l], peer_v[l]),
                        ln2_g[l], ln2_b[l])
        k_p_rows.append(k_rows)
        v_p_rows.append(v_rows)
        conv_p_rows.append(conv_st)

        Ls = hs.shape[1]
        glu, q, k, v, gates = split_projection(hs, w_in[l])
        k_rows = k.reshape(Bd, Ls, N_HEADS, 2 * HEAD_DIM)
        v_rows = v.reshape(Bd, Ls, N_HEADS, V_DIM)
        conv_out, conv_st = conv_branch(glu, state_conv[l], conv_w[l], conv_b[l], conv_ln_g[l],
                                        conv_ln_b[l], w_conv_out[l])
        k_all = jnp.concatenate([cache_k[l].astype(k_rows.dtype), k_rows], axis=1)
        v_all = jnp.concatenate([cache_v[l].astype(v_rows.dtype), v_rows], axis=1)
        o = sample_attention(q.reshape(Bd, Ls, N_HEADS, 2, HEAD_DIM),
                             k_all.reshape(Bd, k_all.shape[1], N_HEADS, 2, HEAD_DIM), v_all, rel_bias, lam)
        hs = merge_and_norm(hs, conv_out, attn_branch_out(o, subln_g[l], lam_init), gates,
                            b_gate[l], w_out[l], ln1_g[l], ln1_b[l])
        hs = layer_norm(ALPHA * hs + peer(hs, w_peer_q[l], sub_keys[l], peer_u[l], peer_v[l]),
                        ln2_g[l], ln2_b[l])
        k_s_rows.append(k_rows)
        v_s_rows.append(v_rows)
        conv_s_rows.append(conv_st)

    return (hp, hs, jnp.stack(k_p_rows), jnp.stack(v_p_rows), jnp.stack(conv_p_rows),
            jnp.stack(k_s_rows), jnp.stack(v_s_rows), jnp.stack(conv_s_rows))
```

```python
import functools
import math

import jax
import jax.numpy as jnp
from jax import lax
from jax.experimental import pallas as pl
from jax.experimental.pallas import tpu as pltpu

F32 = jnp.float32
BF16 = jnp.bfloat16

DEPTH = 1
CHUNK = 64
N_META = 16
CONV_WIDTH = 31
N_HEADS = 8
HEAD_DIM = 64
V_DIM = 2 * HEAD_DIM
NUM_BUCKETS = 32
MAX_DISTANCE = 128
PEER_HEADS = 8
PEER_TOPK = 16
N_KEYS = 128
PEER_HALF = 128
ALPHA = (2.0 * DEPTH) ** 0.25
LN_EPS = 1e-5
NEG_INF = -1e30

SUBLANES = 8
LANES = 128
VMEM_LIMIT = 56 * 1024 * 1024

CTX_ROWS = 32
CTX_PAD = CTX_ROWS - (CONV_WIDTH - 1)


def _layer_norm(x, g, b):
    mu = jnp.mean(x, axis=-1, keepdims=True)
    xc = x - mu
    var = jnp.mean(xc * xc, axis=-1, keepdims=True)
    return xc * lax.rsqrt(var + LN_EPS) * g + b


def _params(*sem):
    return pltpu.CompilerParams(dimension_semantics=sem, vmem_limit_bytes=VMEM_LIMIT)


def _const_spec(shape):
    nd = len(shape)
    return pl.BlockSpec(shape, lambda *_: (0,) * nd)


def _inproj_kernel(x_ref, g_ref, b_ref, w_ref, bg_ref,
                   h_ref, u_ref, q_ref, k_ref, v_ref, kb_ref, vb_ref, gs_ref, *, d):
    h = _layer_norm(x_ref[...], g_ref[...], b_ref[...])
    h_ref[...] = h
    hb = h.astype(BF16)

    def proj(c0, n):
        return jnp.dot(hb, w_ref[:, c0:c0 + n], preferred_element_type=F32)

    u_ref[...] = proj(0, d) * jax.nn.sigmoid(proj(d, d))
    q_ref[...] = (proj(2 * d, d) * (HEAD_DIM ** -0.5)).astype(BF16)
    k = proj(3 * d, d)
    k_ref[...] = k
    kb_ref[...] = k.astype(BF16)
    v = proj(4 * d, d)
    v_ref[...] = v
    vb_ref[...] = v.astype(BF16)
    gs_ref[...] = jax.nn.sigmoid(proj(5 * d, 2 * d) + bg_ref[...])


def _inproj(x, ln_g, ln_b, w_in_b, b_gate, tm):
    r, d = x.shape
    d_in = w_in_b.shape[1]
    row = lambda n: pl.BlockSpec((tm, n), lambda i: (i, 0))
    out_shape = (
        jax.ShapeDtypeStruct((r, d), F32),
        jax.ShapeDtypeStruct((r, d), F32),
        jax.ShapeDtypeStruct((r, d), BF16),
        jax.ShapeDtypeStruct((r, d), F32),
        jax.ShapeDtypeStruct((r, d), F32),
        jax.ShapeDtypeStruct((r, d), BF16),
        jax.ShapeDtypeStruct((r, d), BF16),
        jax.ShapeDtypeStruct((r, 2 * d), F32),
    )
    return pl.pallas_call(
        functools.partial(_inproj_kernel, d=d),
        grid=(r // tm,),
        in_specs=[row(d), _const_spec((1, d)), _const_spec((1, d)),
                  pl.BlockSpec((d, d_in), lambda i: (0, 0), pipeline_mode=pl.Buffered(1)),
                  _const_spec((1, 2 * d))],
        out_specs=(row(d), row(d), row(d), row(d), row(d), row(d), row(d), row(2 * d)),
        out_shape=out_shape,
        compiler_params=_params("parallel"),
        name="in_proj",
    )(x, ln_g, ln_b, w_in_b, b_gate)


CONV_RC = 32
CONV_CC = 256


def _conv_kernel(u_ref, ctx_ref, cw_ref, cb_ref, g_ref, b_ref, w_ref, o_ref, buf_ref, c_ref, *, tt, d):
    @pl.when(pl.program_id(1) == 0)
    def _():
        buf_ref[0:CTX_ROWS, :] = ctx_ref[0]

    buf_ref[CTX_ROWS:CTX_ROWS + tt, :] = u_ref[0]
    for r0 in range(0, tt, CONV_RC):
        for c0 in range(0, d, CONV_CC):
            acc = jnp.broadcast_to(cb_ref[:, c0:c0 + CONV_CC], (CONV_RC, CONV_CC))
            for w in range(CONV_WIDTH):
                s = r0 + CTX_PAD + w
                acc = acc + buf_ref[s:s + CONV_RC, c0:c0 + CONV_CC] * cw_ref[w:w + 1, c0:c0 + CONV_CC]
            c_ref[r0:r0 + CONV_RC, c0:c0 + CONV_CC] = acc
    buf_ref[0:CTX_ROWS, :] = buf_ref[tt:tt + CTX_ROWS, :]
    y = _layer_norm(c_ref[...], g_ref[...], b_ref[...])
    y = y * jax.nn.sigmoid(y)
    o_ref[0] = jnp.dot(y.astype(BF16), w_ref[...], preferred_element_type=F32)


def _conv_branch(u, ctx, cw, cb, ln_g, ln_b, w_b, tt):
    bsz, t, d = u.shape
    nctx = ctx.shape[0]
    ctx_map = (lambda b, i: (b, 0, 0)) if nctx > 1 else (lambda b, i: (0, 0, 0))
    return pl.pallas_call(
        functools.partial(_conv_kernel, tt=tt, d=d),
        grid=(bsz, t // tt),
        in_specs=[pl.BlockSpec((1, tt, d), lambda b, i: (b, i, 0)),
                  pl.BlockSpec((1, CTX_ROWS, d), ctx_map),
                  _const_spec((CTX_ROWS, d)), _const_spec((1, d)), _const_spec((1, d)), _const_spec((1, d)),
                  _const_spec((d, d))],
        out_specs=pl.BlockSpec((1, tt, d), lambda b, i: (b, i, 0)),
        out_shape=jax.ShapeDtypeStruct((bsz, t, d), F32),
        scratch_shapes=[pltpu.VMEM((CTX_ROWS + tt, d), F32), pltpu.VMEM((tt, d), F32)],
        compiler_params=_params("parallel", "arbitrary"),
        name="conv_branch",
    )(u, ctx, cw, cb, ln_g, ln_b, w_b)


def _softmax_step(carry, s, v):
    m_old, l_old, acc_old = carry
    m_new = jnp.maximum(m_old, jnp.max(s, axis=-1, keepdims=True))
    alpha = jnp.exp(m_old - m_new)
    p = jnp.exp(s - m_new)
    l_new = alpha * l_old + jnp.sum(p, axis=-1, keepdims=True)
    acc_new = alpha * acc_old + jnp.dot(p.astype(BF16), v, preferred_element_type=F32)
    return m_new, l_new, acc_new


def _scores(q, k):
    return lax.dot_general(q, k, (((1,), (1,)), ((), ())), preferred_element_type=F32)


def _diff_out(o1, o2, lam, g, out_scale):
    o = o1 - lam * o2
    o = o * lax.rsqrt(jnp.mean(o * o, axis=-1, keepdims=True) + LN_EPS)
    return o * g * out_scale


def _prompt_attn_kernel(sc_ref, q_ref, k_ref, v_ref, km_ref, vm_ref, bias_ref, bm_ref, g_ref, o_ref,
                        *, tq, out_scale):
    qi = pl.program_id(2)
    lam = sc_ref[0]
    q = q_ref[0]
    vm = vm_ref[...]
    outs = []
    for m in range(2):
        lo, hi = m * HEAD_DIM, (m + 1) * HEAD_DIM
        qm = q[:, lo:hi]
        s = _scores(qm, km_ref[:, lo:hi]) + bm_ref[0, jnp.minimum(qi, 1)]
        m0 = jnp.max(s, axis=-1, keepdims=True)
        p = jnp.exp(s - m0)
        carry = (m0, jnp.sum(p, axis=-1, keepdims=True),
                 jnp.dot(p.astype(BF16), vm, preferred_element_type=F32))

        def body(j, carry):
            start = pl.multiple_of(j * tq, tq)
            kb = k_ref[0, pl.ds(start, tq), lo:hi]
            vb = v_ref[0, pl.ds(start, tq), :]
            s = _scores(qm, kb) + bias_ref[0, jnp.clip(j - qi + 2, 0, 2)]
            return _softmax_step(carry, s, vb)

        _, l, acc = lax.fori_loop(0, qi + 1, body, carry)
        outs.append(acc / l)
    o_ref[0] = _diff_out(outs[0], outs[1], lam, g_ref[...], out_scale)


def _prompt_attention(scal, q, kb, vb, km, vm, bias3, bm2, subln_g, tq, out_scale):
    bsz, t, d = q.shape
    return pl.pallas_call(
        functools.partial(_prompt_attn_kernel, tq=tq, out_scale=out_scale),
        grid=(bsz, N_HEADS, t // tq),
        in_specs=[pl.BlockSpec(memory_space=pltpu.SMEM),
                  pl.BlockSpec((1, tq, V_DIM), lambda b, h, i: (b, i, h)),
                  pl.BlockSpec((1, t, V_DIM), lambda b, h, i: (b, 0, h)),
                  pl.BlockSpec((1, t, V_DIM), lambda b, h, i: (b, 0, h)),
                  pl.BlockSpec((N_META, V_DIM), lambda b, h, i: (0, h)),
                  pl.BlockSpec((N_META, V_DIM), lambda b, h, i: (0, h)),
                  pl.BlockSpec((1, 3, tq, tq), lambda b, h, i: (h, 0, 0, 0)),
                  pl.BlockSpec((1, 2, tq, N_META), lambda b, h, i: (h, 0, 0, 0)),
                  _const_spec((1, V_DIM))],
        out_specs=pl.BlockSpec((1, tq, V_DIM), lambda b, h, i: (b, i, h)),
        out_shape=jax.ShapeDtypeStruct((bsz, t, d), F32),
        compiler_params=_params("parallel", "parallel", "arbitrary"),
        name="prompt_attn",
    )(scal, q, kb, vb, km, vm, bias3, bm2, subln_g)


def _sample_attn_kernel(sc_ref, q_ref, kc_ref, vc_ref, kn_ref, vn_ref, bc_ref, bn_ref, g_ref, o_ref,
                        *, out_scale):
    lam = sc_ref[0]
    q = q_ref[0]
    kc = kc_ref[0].astype(BF16)
    vc = vc_ref[0].astype(BF16)
    vn = vn_ref[0]
    outs = []
    for m in range(2):
        lo, hi = m * HEAD_DIM, (m + 1) * HEAD_DIM
        qm = q[:, lo:hi]
        s_c = _scores(qm, kc[:, lo:hi]) + bc_ref[0]
        s_n = _scores(qm, kn_ref[0, :, lo:hi]) + bn_ref[0]
        mx = jnp.maximum(jnp.max(s_c, axis=-1, keepdims=True), jnp.max(s_n, axis=-1, keepdims=True))
        p_c = jnp.exp(s_c - mx)
        p_n = jnp.exp(s_n - mx)
        l = jnp.sum(p_c, axis=-1, keepdims=True) + jnp.sum(p_n, axis=-1, keepdims=True)
        acc = (jnp.dot(p_c.astype(BF16), vc, preferred_element_type=F32)
               + jnp.dot(p_n.astype(BF16), vn, preferred_element_type=F32))
        outs.append(acc / l)
    o_ref[0] = _diff_out(outs[0], outs[1], lam, g_ref[...], out_scale)


def _sample_attention(scal, q, kc, vc, kn, vn, bias_c, bias_n, subln_g, out_scale):
    bsz, ts, d = q.shape
    past = kc.shape[1]
    per_bh = lambda n: pl.BlockSpec((1, n, V_DIM), lambda b, h: (b, 0, h))
    return pl.pallas_call(
        functools.partial(_sample_attn_kernel, out_scale=out_scale),
        grid=(bsz, N_HEADS),
        in_specs=[pl.BlockSpec(memory_space=pltpu.SMEM),
                  per_bh(ts), per_bh(past), per_bh(past), per_bh(ts), per_bh(ts),
                  pl.BlockSpec((1, ts, past), lambda b, h: (h, 0, 0)),
                  pl.BlockSpec((1, ts, ts), lambda b, h: (h, 0, 0)),
                  _const_spec((1, V_DIM))],
        out_specs=per_bh(ts),
        out_shape=jax.ShapeDtypeStruct((bsz, ts, d), F32),
        compiler_params=_params("parallel", "parallel"),
        name="sample_attn",
    )(scal, q, kc, vc, kn, vn, bias_c, bias_n, subln_g)


def _merge_kernel(h_ref, c_ref, a_ref, gs_ref, w_ref, g_ref, b_ref, o_ref, *, d):
    mix = gs_ref[:, 0:d] * c_ref[...] + gs_ref[:, d:2 * d] * a_ref[...]
    y = jnp.dot(mix.astype(BF16), w_ref[...], preferred_element_type=F32)
    o_ref[...] = _layer_norm(ALPHA * h_ref[...] + y, g_ref[...], b_ref[...])


def _merge(h, conv_out, attn_out, gs, w_b, ln_g, ln_b, tm):
    r, d = h.shape
    row = lambda n: pl.BlockSpec((tm, n), lambda i: (i, 0))
    return pl.pallas_call(
        functools.partial(_merge_kernel, d=d),
        grid=(r // tm,),
        in_specs=[row(d), row(d), row(d), row(2 * d), _const_spec((d, d)), _const_spec((1, d)),
                  _const_spec((1, d))],
        out_specs=row(d),
        out_shape=jax.ShapeDtypeStruct((r, d), F32),
        compiler_params=_params("parallel"),
        name="merge_ln1",
    )(h, conv_out, attn_out, gs, w_b, ln_g, ln_b)


def _top16_sorted(s, rowid16):
    work = s
    top = jnp.full(rowid16.shape, -jnp.inf, F32)
    for r in range(PEER_TOPK):
        mx = jnp.max(work, axis=0, keepdims=True)
        top = jnp.where(rowid16 == r, mx, top)
        if r + 1 < PEER_TOPK:
            work = jnp.where(work == mx, -jnp.inf, work)
    return top


def _peer_prep_kernel(x_ref, wq_ref, sk_ref, p_ref, t_ref, q_scr, *, tb):
    xb = x_ref[...].astype(BF16)
    q_scr[...] = _scores(wq_ref[...], xb).astype(BF16)
    rowid16 = lax.broadcasted_iota(jnp.int32, (PEER_TOPK, tb), 0)
    rowid8 = lax.broadcasted_iota(jnp.int32, (SUBLANES, tb), 0)

    def head(h, _):
        base = pl.multiple_of(h * 2 * PEER_HALF, 2 * PEER_HALF)
        s0 = jnp.dot(sk_ref[0], q_scr[pl.ds(base, PEER_HALF), :], preferred_element_type=F32)
        s1 = jnp.dot(sk_ref[1], q_scr[pl.ds(base + PEER_HALF, PEER_HALF), :], preferred_element_type=F32)
        a = _top16_sorted(s0, rowid16)
        b = _top16_sorted(s1, rowid16)
        cands = [a[0:1] + b[0:8], a[0:1] + b[8:16]]
        for i in range(1, 8):
            cands.append(jnp.where(rowid8 < PEER_TOPK // (i + 1), a[i:i + 1] + b[0:8], -jnp.inf))
        cands.append(a[8:16] + b[0:1])
        top = _top16_sorted(jnp.concatenate(cands, axis=0), rowid16)
        z = jnp.sum(jnp.exp(top - top[0:1]), axis=0, keepdims=True)
        p_ref[h, 0] = s0
        p_ref[h, 1] = jnp.exp(s0 - a[0:1]) / z
        p_ref[h, 2] = s1
        p_ref[h, 3] = jnp.exp(s1 - b[0:1])
        t_ref[h] = jnp.broadcast_to(top[PEER_TOPK - 1:PEER_TOPK], (SUBLANES, tb))
        return 0

    lax.fori_loop(0, PEER_HEADS, head, 0)


def _peer_prep(x, wq_t, sub_keys_b, tb):
    r, d = x.shape
    nq = wq_t.shape[0]
    return pl.pallas_call(
        functools.partial(_peer_prep_kernel, tb=tb),
        grid=(r // tb,),
        in_specs=[pl.BlockSpec((tb, d), lambda i: (i, 0)), _const_spec((nq, d)),
                  _const_spec((2, N_KEYS, PEER_HALF))],
        out_specs=(pl.BlockSpec((PEER_HEADS, 4, N_KEYS, tb), lambda i: (0, 0, 0, i)),
                   pl.BlockSpec((PEER_HEADS, SUBLANES, tb), lambda i: (0, 0, i))),
        out_shape=(jax.ShapeDtypeStruct((PEER_HEADS, 4, N_KEYS, r), F32),
                   jax.ShapeDtypeStruct((PEER_HEADS, SUBLANES, r), F32)),
        scratch_shapes=[pltpu.VMEM((nq, tb), BF16)],
        compiler_params=_params("parallel"),
        name="peer_prep",
    )(x, wq_t, sub_keys_b)


def _peer_kernel(x_ref, p_ref, t_ref, u_ref, vt_ref, g_ref, b_ref, o_ref, acc_ref, *, ec, tb):
    e = pl.program_id(1)

    @pl.when(e == 0)
    def _():
        acc_ref[...] = jnp.zeros_like(acc_ref)

    xb = x_ref[...].astype(BF16)
    pre = _scores(u_ref[...], xb)
    act = 0.5 * pre * (1.0 + lax.erf(pre * (2.0 ** -0.5)))
    groups = []
    for gi in range(ec // N_KEYS):
        i = e * (ec // N_KEYS) + gi
        w = jnp.zeros((N_KEYS, tb), F32)
        for h in range(PEER_HEADS):
            c = p_ref[h, 0, pl.ds(i, 1), :] + p_ref[h, 2]
            sel = jnp.where(c >= t_ref[h, 0:1, :], p_ref[h, 3], 0.0)
            w = w + sel * p_ref[h, 1, pl.ds(i, 1), :]
        groups.append(w)
    z = (jnp.concatenate(groups, axis=0) * act).astype(BF16)
    acc_ref[...] += jnp.dot(vt_ref[...], z, preferred_element_type=F32)

    @pl.when(e == pl.num_programs(1) - 1)
    def _():
        y = acc_ref[...].T
        o_ref[...] = _layer_norm(ALPHA * x_ref[...] + y, g_ref[...], b_ref[...])


def _peer(x, pfac, thr, u_b, vt_b, ln_g, ln_b, tb, ec):
    r, d = x.shape
    n_exp = u_b.shape[0]
    return pl.pallas_call(
        functools.partial(_peer_kernel, ec=ec, tb=tb),
        grid=(r // tb, n_exp // ec),
        in_specs=[pl.BlockSpec((tb, d), lambda i, e: (i, 0)),
                  pl.BlockSpec((PEER_HEADS, 4, N_KEYS, tb), lambda i, e: (0, 0, 0, i)),
                  pl.BlockSpec((PEER_HEADS, SUBLANES, tb), lambda i, e: (0, 0, i)),
                  pl.BlockSpec((ec, d), lambda i, e: (e, 0)),
                  pl.BlockSpec((d, ec), lambda i, e: (0, e)),
                  _const_spec((1, d)), _const_spec((1, d))],
        out_specs=pl.BlockSpec((tb, d), lambda i, e: (i, 0)),
        out_shape=jax.ShapeDtypeStruct((r, d), F32),
        scratch_shapes=[pltpu.VMEM((d, tb), F32)],
        compiler_params=_params("parallel", "arbitrary"),
        name="peer_experts",
    )(x, pfac, thr, u_b, vt_b, ln_g, ln_b)


def _t5_bucket(rel):
    half = NUM_BUCKETS // 2
    max_exact = half // 2
    n = jnp.abs(rel)
    large = max_exact + (jnp.log(jnp.maximum(n, 1).astype(F32) / max_exact)
                         / math.log(MAX_DISTANCE / max_exact) * (half - max_exact)).astype(jnp.int32)
    large = jnp.minimum(large, half - 1)
    return jnp.where(rel > 0, half, 0) + jnp.where(n < max_exact, n, large)


def _rel_bias_tile(rel_bias, rel):
    return jnp.transpose(rel_bias[_t5_bucket(rel)], (2, 0, 1)).astype(F32)


def _pick(n, prefs):
    for p in prefs:
        if n % p == 0:
            return p
    raise ValueError(f"no supported tile for extent {n}")


def kernel(x_prompt, x_sample, cache_k, cache_v, state_conv, meta_tokens, ln_in_g, ln_in_b, rel_bias, w_in, b_gate, conv_w, conv_b, conv_ln_g, conv_ln_b, w_conv_out, lam_q1, lam_k1, lam_q2, lam_k2, subln_g, w_out, ln1_g, ln1_b, w_peer_q, sub_keys, peer_u, peer_v, ln2_g, ln2_b):
    bsz, t, d = x_prompt.shape
    bd, ts, _ = x_sample.shape
    past = cache_k.shape[2]
    assert w_in.shape[0] == DEPTH == 1
    assert t >= CONV_WIDTH - 1 and ts >= CONV_WIDTH - 1 and ts % CTX_ROWS == 0
    l = 0
    row2 = lambda a: a.reshape(1, -1)

    lam_init = 0.8 - 0.6 * math.exp(-0.3 * l)
    lam = (jnp.exp(jnp.sum(lam_q1[l] * lam_k1[l]).astype(F32))
           - jnp.exp(jnp.sum(lam_q2[l] * lam_k2[l]).astype(F32)) + lam_init)
    scal = jnp.reshape(lam, (1,)).astype(F32)
    out_scale = 1.0 - lam_init

    w_in_b = w_in[l].astype(BF16)
    ln_g, ln_b = row2(ln_in_g), row2(ln_in_b)
    bg = row2(b_gate[l])

    tm = _pick(bsz * t, (256, 128, 64, 32, 16, 8))
    n_s = bd * ts
    tm_s = 256
    pad = (-(n_s + N_META)) % tm_s
    x_small = jnp.concatenate([x_sample.reshape(n_s, d), meta_tokens.astype(F32), jnp.zeros((pad, d), F32)], axis=0)
    hp, up, qp, kp, vp, kpb, vpb, gsp = _inproj(x_prompt.reshape(bsz * t, d), ln_g, ln_b, w_in_b, bg, tm)
    hs, us, qs, ks, vs, ksb, vsb, gss = _inproj(x_small, ln_g, ln_b, w_in_b, bg, tm_s)
    meta = slice(n_s, n_s + N_META)
    as3 = lambda a, b_: a.reshape(b_, -1, a.shape[-1])

    cw = jnp.concatenate([conv_w[l], jnp.zeros((CTX_ROWS - CONV_WIDTH, d), F32)], axis=0)
    conv_args = (cw, row2(conv_b[l]), row2(conv_ln_g[l]), row2(conv_ln_b[l]), w_conv_out[l].astype(BF16))
    ctx_p = jnp.concatenate([jnp.zeros((CTX_ROWS - N_META, d), F32), us[meta]], axis=0)[None]
    ctx_s = jnp.concatenate([jnp.zeros((bd, CTX_PAD, d), F32), state_conv[l]], axis=1)
    up3 = as3(up, bsz)
    us3 = as3(us[:n_s], bd)
    conv_p = _conv_branch(up3, ctx_p, *conv_args, tt=_pick(t, (256, 128, 64, 32)))
    conv_s = _conv_branch(us3, ctx_s, *conv_args, tt=ts)

    tq = _pick(t, (512, 256, 128, 64))
    assert tq % CHUNK == 0 and tq + 1 >= MAX_DISTANCE
    fq = jnp.arange(tq)
    rel_own = fq[None, :] - fq[:, None]
    own = _rel_bias_tile(rel_bias, rel_own) + jnp.where(fq[None, :] // CHUNK <= fq[:, None] // CHUNK, 0.0, NEG_INF)
    prev = _rel_bias_tile(rel_bias, rel_own - tq)
    far = _rel_bias_tile(rel_bias, jnp.full((1, 1), -(tq + 1)))
    bias3 = jnp.stack([jnp.broadcast_to(far, own.shape), prev, own], axis=1)
    rel_meta = jnp.arange(N_META)[None, :] - (N_META + fq[:, None])
    bm = _rel_bias_tile(rel_bias, rel_meta)
    bm2 = jnp.stack([bm, jnp.broadcast_to(far, bm.shape)], axis=1)
    sg = row2(subln_g[l])
    attn_p = _prompt_attention(scal, as3(qp, bsz), as3(kpb, bsz), as3(vpb, bsz), ksb[meta], vsb[meta],
                               bias3, bm2, sg, tq, out_scale)

    k_pos = jnp.arange(past + ts)
    rel_s = k_pos[None, :] - k_pos[past:, None]
    bias_s = _rel_bias_tile(rel_bias, rel_s)
    attn_s = _sample_attention(scal, as3(qs[:n_s], bd), cache_k[l].reshape(bd, past, d),
                               cache_v[l].reshape(bd, past, d), as3(ksb[:n_s], bd), as3(vsb[:n_s], bd),
                               bias_s[:, :, :past], bias_s[:, :, past:], sg, out_scale)

    w_out_b = w_out[l].astype(BF16)
    h1p = _merge(hp, conv_p.reshape(-1, d), attn_p.reshape(-1, d), gsp, w_out_b, row2(ln1_g[l]), row2(ln1_b[l]), tm)
    h1s = _merge(hs[:n_s], conv_s.reshape(-1, d), attn_s.reshape(-1, d), gss[:n_s], w_out_b,
                 row2(ln1_g[l]), row2(ln1_b[l]), _pick(n_s, (256, 128, 64, 32, 16, 8)))

    wq_t = w_peer_q[l].T.astype(BF16)
    sk_b = sub_keys[l].astype(BF16)
    u_b = peer_u[l].astype(BF16)
    vt_b = peer_v[l].T.astype(BF16)

    def peer_ln2(h1):
        tb = _pick(h1.shape[0], (512, 256, 128))
        pfac, thr = _peer_prep(h1, wq_t, sk_b, _pick(h1.shape[0], (256, 128)))
        return _peer(h1, pfac, thr, u_b, vt_b, row2(ln2_g[l]), row2(ln2_b[l]), tb, ec=512)

    y_p = peer_ln2(h1p).reshape(bsz, t, d)
    y_s = peer_ln2(h1s).reshape(bd, ts, d)

    heads = lambda a, b_: a.reshape(1, b_, -1, N_HEADS, V_DIM)
    with_meta = lambda rows, m: jnp.concatenate([jnp.broadcast_to(m[None], (bsz, N_META, d)), as3(rows, bsz)], axis=1)
    keep = CONV_WIDTH - 1
    return (y_p, y_s,
            heads(with_meta(kp, ks[meta]), bsz), heads(with_meta(vp, vs[meta]), bsz), up3[None, :, t - keep:],
            heads(ks[:n_s], bd), heads(vs[:n_s], bd), us3[None, :, ts - keep:])
```

```python
import functools
import math

import jax
import jax.numpy as jnp
from jax import lax
from jax.experimental import pallas as pl
from jax.experimental.pallas import tpu as pltpu

F32 = jnp.float32
BF16 = jnp.bfloat16

DEPTH = 1
CHUNK = 64
N_META = 16
CONV_WIDTH = 31
N_HEADS = 8
HEAD_DIM = 64
V_DIM = 2 * HEAD_DIM
NUM_BUCKETS = 32
MAX_DISTANCE = 128
PEER_HEADS = 8
PEER_TOPK = 16
N_KEYS = 128
PEER_HALF = 128
ALPHA = (2.0 * DEPTH) ** 0.25
LN_EPS = 1e-5
NEG_INF = -1e30

SUBLANES = 8
LANES = 128
VMEM_LIMIT = 56 * 1024 * 1024

CTX_ROWS = 32
CTX_PAD = CTX_ROWS - (CONV_WIDTH - 1)


def _layer_norm(x, g, b):
    mu = jnp.mean(x, axis=-1, keepdims=True)
    xc = x - mu
    var = jnp.mean(xc * xc, axis=-1, keepdims=True)
    return xc * lax.rsqrt(var + LN_EPS) * g + b


def _params(*sem):
    return pltpu.CompilerParams(dimension_semantics=sem, vmem_limit_bytes=VMEM_LIMIT)


def _const_spec(shape):
    nd = len(shape)
    return pl.BlockSpec(shape, lambda *_: (0,) * nd)


def _inproj_kernel(x_ref, g_ref, b_ref, w_ref, bg_ref,
                   h_ref, u_ref, q_ref, k_ref, v_ref, kb_ref, vb_ref, gs_ref, *, d):
    h = _layer_norm(x_ref[...], g_ref[...], b_ref[...])
    h_ref[...] = h
    hb = h.astype(BF16)

    def proj(c0, n):
        return jnp.dot(hb, w_ref[:, c0:c0 + n], preferred_element_type=F32)

    u_ref[...] = proj(0, d) * jax.nn.sigmoid(proj(d, d))
    q_ref[...] = (proj(2 * d, d) * (HEAD_DIM ** -0.5)).astype(BF16)
    k = proj(3 * d, d)
    k_ref[...] = k
    kb_ref[...] = k.astype(BF16)
    v = proj(4 * d, d)
    v_ref[...] = v
    vb_ref[...] = v.astype(BF16)
    gs_ref[...] = jax.nn.sigmoid(proj(5 * d, 2 * d) + bg_ref[...])


def _inproj(x, ln_g, ln_b, w_in_b, b_gate, tm):
    r, d = x.shape
    d_in = w_in_b.shape[1]
    row = lambda n: pl.BlockSpec((tm, n), lambda i: (i, 0))
    out_shape = (
        jax.ShapeDtypeStruct((r, d), F32),
        jax.ShapeDtypeStruct((r, d), F32),
        jax.ShapeDtypeStruct((r, d), BF16),
        jax.ShapeDtypeStruct((r, d), F32),
        jax.ShapeDtypeStruct((r, d), F32),
        jax.ShapeDtypeStruct((r, d), BF16),
        jax.ShapeDtypeStruct((r, d), BF16),
        jax.ShapeDtypeStruct((r, 2 * d), F32),
    )
    return pl.pallas_call(
        functools.partial(_inproj_kernel, d=d),
        grid=(r // tm,),
        in_specs=[row(d), _const_spec((1, d)), _const_spec((1, d)),
                  pl.BlockSpec((d, d_in), lambda i: (0, 0), pipeline_mode=pl.Buffered(1)),
                  _const_spec((1, 2 * d))],
        out_specs=(row(d), row(d), row(d), row(d), row(d), row(d), row(d), row(2 * d)),
        out_shape=out_shape,
        compiler_params=_params("parallel"),
        name="in_proj",
    )(x, ln_g, ln_b, w_in_b, b_gate)


CONV_RC = 32
CONV_CC = 256


def _conv_kernel(u_ref, ctx_ref, cw_ref, cb_ref, g_ref, b_ref, w_ref, o_ref, buf_ref, c_ref, *, tt, d):
    @pl.when(pl.program_id(1) == 0)
    def _():
        buf_ref[0:CTX_ROWS, :] = ctx_ref[0]

    buf_ref[CTX_ROWS:CTX_ROWS + tt, :] = u_ref[0]
    for r0 in range(0, tt, CONV_RC):
        for c0 in range(0, d, CONV_CC):
            acc = jnp.broadcast_to(cb_ref[:, c0:c0 + CONV_CC], (CONV_RC, CONV_CC))
            for w in range(CONV_WIDTH):
                s = r0 + CTX_PAD + w
                acc = acc + buf_ref[s:s + CONV_RC, c0:c0 + CONV_CC] * cw_ref[w:w + 1, c0:c0 + CONV_CC]
            c_ref[r0:r0 + CONV_RC, c0:c0 + CONV_CC] = acc
    buf_ref[0:CTX_ROWS, :] = buf_ref[tt:tt + CTX_ROWS, :]
    y = _layer_norm(c_ref[...], g_ref[...], b_ref[...])
    y = y * jax.nn.sigmoid(y)
    o_ref[0] = jnp.dot(y.astype(BF16), w_ref[...], preferred_element_type=F32)


def _conv_branch(u, ctx, cw, cb, ln_g, ln_b, w_b, tt):
    bsz, t, d = u.shape
    nctx = ctx.shape[0]
    ctx_map = (lambda b, i: (b, 0, 0)) if nctx > 1 else (lambda b, i: (0, 0, 0))
    return pl.pallas_call(
        functools.partial(_conv_kernel, tt=tt, d=d),
        grid=(bsz, t // tt),
        in_specs=[pl.BlockSpec((1, tt, d), lambda b, i: (b, i, 0)),
                  pl.BlockSpec((1, CTX_ROWS, d), ctx_map),
                  _const_spec((CTX_ROWS, d)), _const_spec((1, d)), _const_spec((1, d)), _const_spec((1, d)),
                  _const_spec((d, d))],
        out_specs=pl.BlockSpec((1, tt, d), lambda b, i: (b, i, 0)),
        out_shape=jax.ShapeDtypeStruct((bsz, t, d), F32),
        scratch_shapes=[pltpu.VMEM((CTX_ROWS + tt, d), F32), pltpu.VMEM((tt, d), F32)],
        compiler_params=_params("parallel", "arbitrary"),
        name="conv_branch",
    )(u, ctx, cw, cb, ln_g, ln_b, w_b)


ATTN_RC = 32


def _softmax_step(carry, s, v):
    m_old, l_old, acc_old = carry
    m_new = jnp.maximum(m_old, jnp.max(s, axis=-1, keepdims=True))
    alpha = jnp.exp(m_old - m_new)
    p = jnp.exp(s - m_new)
    l_new = alpha * l_old + jnp.sum(p, axis=-1, keepdims=True)
    acc_new = alpha * acc_old + jnp.dot(p.astype(BF16), v, preferred_element_type=F32)
    return m_new, l_new, acc_new


def _scores(q, k):
    return lax.dot_general(q, k, (((1,), (1,)), ((), ())), preferred_element_type=F32)


def _diff_out(o1, o2, lam, g, out_scale):
    o = o1 - lam * o2
    o = o * lax.rsqrt(jnp.mean(o * o, axis=-1, keepdims=True) + LN_EPS)
    return o * g * out_scale


def _prompt_attn_kernel(sc_ref, q_ref, k_ref, v_ref, km_ref, vm_ref, bias_ref, bm_ref, g_ref, o_ref,
                        m_scr, acc_scr, *, tq, out_scale):
    qi = pl.program_id(2)
    lam = sc_ref[0]
    far = sc_ref[1 + pl.program_id(1)]
    vm = vm_ref[...]
    halves = [(m * HEAD_DIM, (m + 1) * HEAD_DIM) for m in range(2)]
    n_far = jnp.maximum(qi - 1, 0)
    q = q_ref[0]
    qs = [q[:, lo:hi] for lo, hi in halves]

    def with_ones(v):
        return jnp.concatenate([v, jnp.ones((v.shape[0], LANES), BF16)], axis=1)

    def lane_tiles(x, n):
        return jnp.concatenate([x] * n, axis=1)

    def row_max(s):
        return jnp.broadcast_to(jnp.max(s, axis=-1, keepdims=True), (s.shape[0], LANES))

    for m, (qm, (lo, hi)) in enumerate(zip(qs, halves)):
        s = _scores(qm, km_ref[:, lo:hi]) + bm_ref[0, jnp.minimum(qi, 1)]
        m0 = row_max(s)
        p = jnp.exp(s - m0[:, 0:N_META])
        m_scr[m] = m0
        acc_scr[m] = jnp.dot(p.astype(BF16), with_ones(vm), preferred_element_type=F32)

    def block(j, near):
        start = pl.multiple_of(j * tq, tq)
        vb = with_ones(v_ref[0, pl.ds(start, tq), :])
        shift = 0.0 if near else far
        ss, mbs = [], []
        for m, (qm, (lo, hi)) in enumerate(zip(qs, halves)):
            s = _scores(qm, k_ref[0, pl.ds(start, tq), lo:hi])
            if near:
                s = s + bias_ref[0, j - qi + 1]
            m_old = m_scr[m]
            m_new = jnp.maximum(m_old, row_max(s) + shift)
            m_scr[m] = m_new
            acc_scr[m] = lane_tiles(jnp.exp(m_old - m_new), 2) * acc_scr[m]
            ss.append(s)
            mbs.append(lane_tiles(m_new - shift, tq // LANES))
        for m in range(2):
            ps = []
            for c in range(tq // ATTN_RC):
                rs = slice(c * ATTN_RC, (c + 1) * ATTN_RC)
                ps.append(jnp.exp(ss[m][rs, :] - mbs[m][rs, :]).astype(BF16))
            acc_scr[m] += jnp.dot(jnp.concatenate(ps, axis=0), vb, preferred_element_type=F32)

    def far_body(j, _):
        block(j, False)
        return 0

    def near_body(j, _):
        block(j, True)
        return 0

    lax.fori_loop(0, n_far, far_body, 0)
    lax.fori_loop(n_far, qi + 1, near_body, 0)
    outs = [acc_scr[m, :, 0:V_DIM] / acc_scr[m, :, V_DIM:2 * V_DIM] for m in range(2)]
    o_ref[0] = _diff_out(outs[0], outs[1], lam, g_ref[...], out_scale)


def _prompt_attention(scal, q, kb, vb, km, vm, bias3, bm2, subln_g, tq, out_scale):
    bsz, t, d = q.shape
    return pl.pallas_call(
        functools.partial(_prompt_attn_kernel, tq=tq, out_scale=out_scale),
        grid=(bsz, N_HEADS, t // tq),
        in_specs=[pl.BlockSpec(memory_space=pltpu.SMEM),
                  pl.BlockSpec((1, tq, V_DIM), lambda b, h, i: (b, i, h)),
                  pl.BlockSpec((1, t, V_DIM), lambda b, h, i: (b, 0, h)),
                  pl.BlockSpec((1, t, V_DIM), lambda b, h, i: (b, 0, h)),
                  pl.BlockSpec((N_META, V_DIM), lambda b, h, i: (0, h)),
                  pl.BlockSpec((N_META, V_DIM), lambda b, h, i: (0, h)),
                  pl.BlockSpec((1, 2, tq, tq), lambda b, h, i: (h, 0, 0, 0)),
                  pl.BlockSpec((1, 2, tq, N_META), lambda b, h, i: (h, 0, 0, 0)),
                  _const_spec((1, V_DIM))],
        out_specs=pl.BlockSpec((1, tq, V_DIM), lambda b, h, i: (b, i, h)),
        out_shape=jax.ShapeDtypeStruct((bsz, t, d), F32),
        scratch_shapes=[pltpu.VMEM((2, tq, LANES), F32), pltpu.VMEM((2, tq, V_DIM + LANES), F32)],
        compiler_params=_params("parallel", "parallel", "arbitrary"),
        name="prompt_attn",
    )(scal, q, kb, vb, km, vm, bias3, bm2, subln_g)


def _sample_attn_kernel(sc_ref, q_ref, kc_ref, vc_ref, kn_ref, vn_ref, bc_ref, bn_ref, g_ref, o_ref,
                        *, out_scale):
    lam = sc_ref[0]
    q = q_ref[0]
    kc = kc_ref[0].astype(BF16)
    vc = vc_ref[0].astype(BF16)
    vn = vn_ref[0]
    outs = []
    for m in range(2):
        lo, hi = m * HEAD_DIM, (m + 1) * HEAD_DIM
        qm = q[:, lo:hi]
        s_c = _scores(qm, kc[:, lo:hi]) + bc_ref[0]
        s_n = _scores(qm, kn_ref[0, :, lo:hi]) + bn_ref[0]
        mx = jnp.maximum(jnp.max(s_c, axis=-1, keepdims=True), jnp.max(s_n, axis=-1, keepdims=True))
        p_c = jnp.exp(s_c - mx)
        p_n = jnp.exp(s_n - mx)
        l = jnp.sum(p_c, axis=-1, keepdims=True) + jnp.sum(p_n, axis=-1, keepdims=True)
        acc = (jnp.dot(p_c.astype(BF16), vc, preferred_element_type=F32)
               + jnp.dot(p_n.astype(BF16), vn, preferred_element_type=F32))
        outs.append(acc / l)
    o_ref[0] = _diff_out(outs[0], outs[1], lam, g_ref[...], out_scale)


def _sample_attention(scal, q, kc, vc, kn, vn, bias_c, bias_n, subln_g, out_scale):
    bsz, ts, d = q.shape
    past = kc.shape[1]
    per_bh = lambda n: pl.BlockSpec((1, n, V_DIM), lambda b, h: (b, 0, h))
    return pl.pallas_call(
        functools.partial(_sample_attn_kernel, out_scale=out_scale),
        grid=(bsz, N_HEADS),
        in_specs=[pl.BlockSpec(memory_space=pltpu.SMEM),
                  per_bh(ts), per_bh(past), per_bh(past), per_bh(ts), per_bh(ts),
                  pl.BlockSpec((1, ts, past), lambda b, h: (h, 0, 0)),
                  pl.BlockSpec((1, ts, ts), lambda b, h: (h, 0, 0)),
                  _const_spec((1, V_DIM))],
        out_specs=per_bh(ts),
        out_shape=jax.ShapeDtypeStruct((bsz, ts, d), F32),
        compiler_params=_params("parallel", "parallel"),
        name="sample_attn",
    )(scal, q, kc, vc, kn, vn, bias_c, bias_n, subln_g)


def _merge_kernel(h_ref, c_ref, a_ref, gs_ref, w_ref, g_ref, b_ref, o_ref, ob_ref, *, d):
    mix = gs_ref[:, 0:d] * c_ref[...] + gs_ref[:, d:2 * d] * a_ref[...]
    y = jnp.dot(mix.astype(BF16), w_ref[...], preferred_element_type=F32)
    h1 = _layer_norm(ALPHA * h_ref[...] + y, g_ref[...], b_ref[...])
    o_ref[...] = h1
    ob_ref[...] = h1.astype(BF16)


def _merge(h, conv_out, attn_out, gs, w_b, ln_g, ln_b, tm):
    r, d = h.shape
    row = lambda n: pl.BlockSpec((tm, n), lambda i: (i, 0))
    return pl.pallas_call(
        functools.partial(_merge_kernel, d=d),
        grid=(r // tm,),
        in_specs=[row(d), row(d), row(d), row(2 * d), _const_spec((d, d)), _const_spec((1, d)),
                  _const_spec((1, d))],
        out_specs=(row(d), row(d)),
        out_shape=(jax.ShapeDtypeStruct((r, d), F32), jax.ShapeDtypeStruct((r, d), BF16)),
        compiler_params=_params("parallel"),
        name="merge_ln1",
    )(h, conv_out, attn_out, gs, w_b, ln_g, ln_b)


def _top16_sorted(s, rowid16):
    work = s
    top = jnp.full(rowid16.shape, -jnp.inf, F32)
    for r in range(PEER_TOPK):
        mx = jnp.max(work, axis=0, keepdims=True)
        top = jnp.where(rowid16 == r, mx, top)
        if r + 1 < PEER_TOPK:
            work = jnp.where(work == mx, -jnp.inf, work)
    return top


def _peer_prep_kernel(x_ref, wq_ref, sk_ref, p_ref, t_ref, q_scr, *, tb):
    q_scr[...] = _scores(wq_ref[...], x_ref[...]).astype(BF16)
    rowid16 = lax.broadcasted_iota(jnp.int32, (PEER_TOPK, tb), 0)
    rowid8 = lax.broadcasted_iota(jnp.int32, (SUBLANES, tb), 0)

    def head(h, _):
        base = pl.multiple_of(h * 2 * PEER_HALF, 2 * PEER_HALF)
        s0 = jnp.dot(sk_ref[0], q_scr[pl.ds(base, PEER_HALF), :], preferred_element_type=F32)
        s1 = jnp.dot(sk_ref[1], q_scr[pl.ds(base + PEER_HALF, PEER_HALF), :], preferred_element_type=F32)
        a = _top16_sorted(s0, rowid16)
        b = _top16_sorted(s1, rowid16)
        cands = [a[0:1] + b[0:8], a[0:1] + b[8:16]]
        for i in range(1, 8):
            cands.append(jnp.where(rowid8 < PEER_TOPK // (i + 1), a[i:i + 1] + b[0:8], -jnp.inf))
        cands.append(a[8:16] + b[0:1])
        top = _top16_sorted(jnp.concatenate(cands, axis=0), rowid16)
        z = jnp.sum(jnp.exp(top - top[0:1]), axis=0, keepdims=True)
        p_ref[h, 0] = s0
        p_ref[h, 1] = jnp.exp(s0 - a[0:1]) / z
        p_ref[h, 2] = s1
        p_ref[h, 3] = jnp.exp(s1 - b[0:1])
        t_ref[h] = jnp.broadcast_to(top[PEER_TOPK - 1:PEER_TOPK], (SUBLANES, tb))
        return 0

    lax.fori_loop(0, PEER_HEADS, head, 0)


def _peer_prep(x, wq_t, sub_keys_b, tb):
    r, d = x.shape
    nq = wq_t.shape[0]
    return pl.pallas_call(
        functools.partial(_peer_prep_kernel, tb=tb),
        grid=(r // tb,),
        in_specs=[pl.BlockSpec((tb, d), lambda i: (i, 0)), _const_spec((nq, d)),
                  _const_spec((2, N_KEYS, PEER_HALF))],
        out_specs=(pl.BlockSpec((PEER_HEADS, 4, N_KEYS, tb), lambda i: (0, 0, 0, i)),
                   pl.BlockSpec((PEER_HEADS, SUBLANES, tb), lambda i: (0, 0, i))),
        out_shape=(jax.ShapeDtypeStruct((PEER_HEADS, 4, N_KEYS, r), F32),
                   jax.ShapeDtypeStruct((PEER_HEADS, SUBLANES, r), F32)),
        scratch_shapes=[pltpu.VMEM((nq, tb), BF16)],
        compiler_params=_params("parallel"),
        name="peer_prep",
    )(x, wq_t, sub_keys_b)


PEER_JT = 64
PEER_SUB = 256


def _peer_kernel(x_ref, xb_ref, p_ref, t_ref, u_ref, vt_ref, g_ref, b_ref, o_ref, acc_ref, *, ec, tb):
    e = pl.program_id(1)

    @pl.when(e == 0)
    def _():
        acc_ref[...] = jnp.zeros_like(acc_ref)

    xb = xb_ref[...]
    n_g = PEER_SUB // N_KEYS
    y_t = None
    for sc in range(ec // PEER_SUB):
        sub = slice(sc * PEER_SUB, (sc + 1) * PEER_SUB)
        pre_sub = _scores(u_ref[sub, :], xb)
        i0 = (e * (ec // PEER_SUB) + sc) * n_g
        s0_rows = [[p_ref[h, 0, pl.ds(i0 + gi, 1), :] for h in range(PEER_HEADS)] for gi in range(n_g)]
        ea_rows = [[p_ref[h, 1, pl.ds(i0 + gi, 1), :] for h in range(PEER_HEADS)] for gi in range(n_g)]
        z_tiles = [[[None] * (tb // LANES) for _ in range(N_KEYS // PEER_JT)] for _ in range(n_g)]
        for lc in range(tb // LANES):
            lanes = slice(lc * LANES, (lc + 1) * LANES)
            for jt in range(N_KEYS // PEER_JT):
                js = slice(jt * PEER_JT, (jt + 1) * PEER_JT)
                w = [jnp.zeros((PEER_JT, LANES), F32) for _ in range(n_g)]
                for h in range(PEER_HEADS):
                    s1 = p_ref[h, 2, js, lanes]
                    eb = p_ref[h, 3, js, lanes]
                    thr = t_ref[h, 0:1, lanes]
                    for gi in range(n_g):
                        sel = jnp.where(s0_rows[gi][h][:, lanes] + s1 >= thr, eb, 0.0)
                        w[gi] = w[gi] + sel * ea_rows[gi][h][:, lanes]
                for gi in range(n_g):
                    pre = pre_sub[gi * N_KEYS + jt * PEER_JT:gi * N_KEYS + (jt + 1) * PEER_JT, lanes]
                    act = 0.5 * pre * (1.0 + lax.erf(pre * (2.0 ** -0.5)))
                    z_tiles[gi][jt][lc] = (w[gi] * act).astype(BF16)
        z = jnp.concatenate([jnp.concatenate(t, axis=1) for g in z_tiles for t in g], axis=0)
        part = jnp.dot(vt_ref[:, sub], z, preferred_element_type=F32)
        y_t = part if y_t is None else y_t + part
    acc_ref[...] += y_t

    @pl.when(e == pl.num_programs(1) - 1)
    def _():
        y = acc_ref[...].T
        o_ref[...] = _layer_norm(ALPHA * x_ref[...] + y, g_ref[...], b_ref[...])


def _peer(x, xb, pfac, thr, u_b, vt_b, ln_g, ln_b, tb, ec):
    r, d = x.shape
    n_exp = u_b.shape[0]
    return pl.pallas_call(
        functools.partial(_peer_kernel, ec=ec, tb=tb),
        grid=(r // tb, n_exp // ec),
        in_specs=[pl.BlockSpec((tb, d), lambda i, e: (i, 0)),
                  pl.BlockSpec((tb, d), lambda i, e: (i, 0)),
                  pl.BlockSpec((PEER_HEADS, 4, N_KEYS, tb), lambda i, e: (0, 0, 0, i)),
                  pl.BlockSpec((PEER_HEADS, SUBLANES, tb), lambda i, e: (0, 0, i)),
                  pl.BlockSpec((ec, d), lambda i, e: (e, 0)),
                  pl.BlockSpec((d, ec), lambda i, e: (0, e)),
                  _const_spec((1, d)), _const_spec((1, d))],
        out_specs=pl.BlockSpec((tb, d), lambda i, e: (i, 0)),
        out_shape=jax.ShapeDtypeStruct((r, d), F32),
        scratch_shapes=[pltpu.VMEM((d, tb), F32)],
        compiler_params=_params("parallel", "arbitrary"),
        name="peer_experts",
    )(x, xb, pfac, thr, u_b, vt_b, ln_g, ln_b)


def _t5_bucket(rel):
    half = NUM_BUCKETS // 2
    max_exact = half // 2
    n = jnp.abs(rel)
    large = max_exact + (jnp.log(jnp.maximum(n, 1).astype(F32) / max_exact)
                         / math.log(MAX_DISTANCE / max_exact) * (half - max_exact)).astype(jnp.int32)
    large = jnp.minimum(large, half - 1)
    return jnp.where(rel > 0, half, 0) + jnp.where(n < max_exact, n, large)


def _rel_bias_tile(rel_bias, rel):
    bucket = _t5_bucket(rel)[None]
    tile = jnp.zeros((rel_bias.shape[1],) + rel.shape, F32)
    for b in range(NUM_BUCKETS):
        tile = jnp.where(bucket == b, rel_bias[b].astype(F32)[:, None, None], tile)
    return tile


def _pick(n, prefs):
    for p in prefs:
        if n % p == 0:
            return p
    raise ValueError(f"no supported tile for extent {n}")


def kernel(x_prompt, x_sample, cache_k, cache_v, state_conv, meta_tokens, ln_in_g, ln_in_b, rel_bias, w_in, b_gate, conv_w, conv_b, conv_ln_g, conv_ln_b, w_conv_out, lam_q1, lam_k1, lam_q2, lam_k2, subln_g, w_out, ln1_g, ln1_b, w_peer_q, sub_keys, peer_u, peer_v, ln2_g, ln2_b):
    bsz, t, d = x_prompt.shape
    bd, ts, _ = x_sample.shape
    past = cache_k.shape[2]
    assert w_in.shape[0] == DEPTH == 1
    assert t >= CONV_WIDTH - 1 and ts >= CONV_WIDTH - 1 and ts % CTX_ROWS == 0
    l = 0
    row2 = lambda a: a.reshape(1, -1)

    lam_init = 0.8 - 0.6 * math.exp(-0.3 * l)
    lam = (jnp.exp(jnp.sum(lam_q1[l] * lam_k1[l]).astype(F32))
           - jnp.exp(jnp.sum(lam_q2[l] * lam_k2[l]).astype(F32)) + lam_init)
    scal = jnp.reshape(lam, (1,)).astype(F32)
    out_scale = 1.0 - lam_init

    w_in_b = w_in[l].astype(BF16)
    ln_g, ln_b = row2(ln_in_g), row2(ln_in_b)
    bg = row2(b_gate[l])

    tm = _pick(bsz * t, (256, 128, 64, 32, 16, 8))
    n_s = bd * ts
    tm_s = 256
    pad = (-(n_s + N_META)) % tm_s
    x_small = jnp.concatenate([x_sample.reshape(n_s, d), meta_tokens.astype(F32), jnp.zeros((pad, d), F32)], axis=0)
    hp, up, qp, kp, vp, kpb, vpb, gsp = _inproj(x_prompt.reshape(bsz * t, d), ln_g, ln_b, w_in_b, bg, tm)
    hs, us, qs, ks, vs, ksb, vsb, gss = _inproj(x_small, ln_g, ln_b, w_in_b, bg, tm_s)
    meta = slice(n_s, n_s + N_META)
    as3 = lambda a, b_: a.reshape(b_, -1, a.shape[-1])

    cw = jnp.concatenate([conv_w[l], jnp.zeros((CTX_ROWS - CONV_WIDTH, d), F32)], axis=0)
    conv_args = (cw, row2(conv_b[l]), row2(conv_ln_g[l]), row2(conv_ln_b[l]), w_conv_out[l].astype(BF16))
    ctx_p = jnp.concatenate([jnp.zeros((CTX_ROWS - N_META, d), F32), us[meta]], axis=0)[None]
    ctx_s = jnp.concatenate([jnp.zeros((bd, CTX_PAD, d), F32), state_conv[l]], axis=1)
    up3 = as3(up, bsz)
    us3 = as3(us[:n_s], bd)
    conv_p = _conv_branch(up3, ctx_p, *conv_args, tt=_pick(t, (256, 128, 64, 32)))
    conv_s = _conv_branch(us3, ctx_s, *conv_args, tt=ts)

    tq = _pick(t, (512, 256, 128, 64))
    assert tq % CHUNK == 0 and tq + 1 >= MAX_DISTANCE
    fq = jnp.arange(tq)
    rel_own = fq[None, :] - fq[:, None]
    own = _rel_bias_tile(rel_bias, rel_own) + jnp.where(fq[None, :] // CHUNK <= fq[:, None] // CHUNK, 0.0, NEG_INF)
    prev = _rel_bias_tile(rel_bias, rel_own - tq)
    far = _rel_bias_tile(rel_bias, jnp.full((1, 1), -(tq + 1)))
    bias3 = jnp.stack([prev, own], axis=1)
    scal = jnp.concatenate([scal, far.reshape(N_HEADS)])
    rel_meta = jnp.arange(N_META)[None, :] - (N_META + fq[:, None])
    bm = _rel_bias_tile(rel_bias, rel_meta)
    bm2 = jnp.stack([bm, jnp.broadcast_to(far, bm.shape)], axis=1)
    sg = row2(subln_g[l])
    attn_p = _prompt_attention(scal, as3(qp, bsz), as3(kpb, bsz), as3(vpb, bsz), ksb[meta], vsb[meta],
                               bias3, bm2, sg, tq, out_scale)

    k_pos = jnp.arange(past + ts)
    rel_s = k_pos[None, :] - k_pos[past:, None]
    bias_s = _rel_bias_tile(rel_bias, rel_s)
    attn_s = _sample_attention(scal, as3(qs[:n_s], bd), cache_k[l].reshape(bd, past, d),
                               cache_v[l].reshape(bd, past, d), as3(ksb[:n_s], bd), as3(vsb[:n_s], bd),
                               bias_s[:, :, :past], bias_s[:, :, past:], sg, out_scale)

    w_out_b = w_out[l].astype(BF16)
    h1p = _merge(hp, conv_p.reshape(-1, d), attn_p.reshape(-1, d), gsp, w_out_b, row2(ln1_g[l]), row2(ln1_b[l]), tm)
    h1s = _merge(hs[:n_s], conv_s.reshape(-1, d), attn_s.reshape(-1, d), gss[:n_s], w_out_b,
                 row2(ln1_g[l]), row2(ln1_b[l]), _pick(n_s, (256, 128, 64, 32, 16, 8)))

    wq_t = w_peer_q[l].T.astype(BF16)
    sk_b = sub_keys[l].astype(BF16)
    u_b = peer_u[l].astype(BF16)
    vt_b = peer_v[l].T.astype(BF16)

    def peer_ln2(h1, h1b):
        tb = _pick(h1.shape[0], (512, 256, 128))
        pfac, thr = _peer_prep(h1b, wq_t, sk_b, _pick(h1.shape[0], (256, 128)))
        return _peer(h1, h1b, pfac, thr, u_b, vt_b, row2(ln2_g[l]), row2(ln2_b[l]), tb, ec=1024)

    y_p = peer_ln2(*h1p).reshape(bsz, t, d)
    y_s = peer_ln2(*h1s).reshape(bd, ts, d)

    heads = lambda a, b_: a.reshape(1, b_, -1, N_HEADS, V_DIM)
    with_meta = lambda rows, m: jnp.concatenate([jnp.broadcast_to(m[None], (bsz, N_META, d)), as3(rows, bsz)], axis=1)
    keep = CONV_WIDTH - 1
    return (y_p, y_s,
            heads(with_meta(kp, ks[meta]), bsz), heads(with_meta(vp, vs[meta]), bsz), up3[None, :, t - keep:],
            heads(ks[:n_s], bd), heads(vs[:n_s], bd), us3[None, :, ts - keep:])
```

```python
import functools
import math

import jax
import jax.numpy as jnp
from jax import lax
from jax.experimental import pallas as pl
from jax.experimental.pallas import tpu as pltpu

F32 = jnp.float32
BF16 = jnp.bfloat16

DEPTH = 1
CHUNK = 64
N_META = 16
CONV_WIDTH = 31
N_HEADS = 8
HEAD_DIM = 64
V_DIM = 2 * HEAD_DIM
NUM_BUCKETS = 32
MAX_DISTANCE = 128
PEER_HEADS = 8
PEER_TOPK = 16
N_KEYS = 128
PEER_HALF = 128
ALPHA = (2.0 * DEPTH) ** 0.25
LN_EPS = 1e-5
NEG_INF = -1e30

SUBLANES = 8
LANES = 128
MXU_TILE = 256
VMEM_LIMIT = 56 * 1024 * 1024

CTX_ROWS = 32
CTX_PAD = CTX_ROWS - (CONV_WIDTH - 1)


def _layer_norm(x, g, b):
    mu = jnp.mean(x, axis=-1, keepdims=True)
    xc = x - mu
    var = jnp.mean(xc * xc, axis=-1, keepdims=True)
    return xc * lax.rsqrt(var + LN_EPS) * g + b


def _params(*sem):
    return pltpu.CompilerParams(dimension_semantics=sem, vmem_limit_bytes=VMEM_LIMIT)


def _const_spec(shape):
    nd = len(shape)
    return pl.BlockSpec(shape, lambda *_: (0,) * nd)


def _inproj_kernel(x_ref, g_ref, b_ref, w_ref, bg_ref,
                   h_ref, u_ref, q_ref, k_ref, v_ref, kb_ref, vb_ref, gs_ref, *, d):
    h = _layer_norm(x_ref[...], g_ref[...], b_ref[...])
    h_ref[...] = h
    hb = h.astype(BF16)

    def proj(c0, n):
        return jnp.dot(hb, w_ref[:, c0:c0 + n], preferred_element_type=F32)

    u_ref[...] = proj(0, d) * jax.nn.sigmoid(proj(d, d))
    q_ref[...] = (proj(2 * d, d) * (HEAD_DIM ** -0.5)).astype(BF16)
    k = proj(3 * d, d)
    k_ref[...] = k
    kb_ref[...] = k.astype(BF16)
    v = proj(4 * d, d)
    v_ref[...] = v
    vb_ref[...] = v.astype(BF16)
    gs_ref[...] = jax.nn.sigmoid(proj(5 * d, 2 * d) + bg_ref[...])


def _inproj(x, ln_g, ln_b, w_in_b, b_gate, tm):
    r, d = x.shape
    d_in = w_in_b.shape[1]
    row = lambda n: pl.BlockSpec((tm, n), lambda i: (i, 0))
    out_shape = (
        jax.ShapeDtypeStruct((r, d), F32),
        jax.ShapeDtypeStruct((r, d), F32),
        jax.ShapeDtypeStruct((r, d), BF16),
        jax.ShapeDtypeStruct((r, d), F32),
        jax.ShapeDtypeStruct((r, d), F32),
        jax.ShapeDtypeStruct((r, d), BF16),
        jax.ShapeDtypeStruct((r, d), BF16),
        jax.ShapeDtypeStruct((r, 2 * d), F32),
    )
    return pl.pallas_call(
        functools.partial(_inproj_kernel, d=d),
        grid=(r // tm,),
        in_specs=[row(d), _const_spec((1, d)), _const_spec((1, d)),
                  pl.BlockSpec((d, d_in), lambda i: (0, 0), pipeline_mode=pl.Buffered(1)),
                  _const_spec((1, 2 * d))],
        out_specs=(row(d), row(d), row(d), row(d), row(d), row(d), row(d), row(2 * d)),
        out_shape=out_shape,
        compiler_params=_params("parallel"),
        name="in_proj",
    )(x, ln_g, ln_b, w_in_b, b_gate)


CONV_RC = 32
CONV_CC = 256


def _conv_kernel(u_ref, ctx_ref, cw_ref, cb_ref, g_ref, b_ref, w_ref, o_ref, buf_ref, sh_ref, c_ref, *, tt, d):
    @pl.when(pl.program_id(1) == 0)
    def _():
        buf_ref[0:CTX_ROWS, :] = ctx_ref[0]

    buf_ref[CTX_ROWS:CTX_ROWS + tt, :] = u_ref[0]
    n_sh = tt + CTX_ROWS - SUBLANES
    for r in range(1, SUBLANES):
        sh_ref[r - 1, 0:n_sh, :] = buf_ref[r:r + n_sh, :]
    for r0 in range(0, tt, CONV_RC):
        for c0 in range(0, d, CONV_CC):
            acc = jnp.broadcast_to(cb_ref[:, c0:c0 + CONV_CC], (CONV_RC, CONV_CC))
            for w in range(CONV_WIDTH):
                a, r = divmod(CTX_PAD + w, SUBLANES)
                s = r0 + a * SUBLANES
                rows = (buf_ref[s:s + CONV_RC, c0:c0 + CONV_CC] if r == 0
                        else sh_ref[r - 1, s:s + CONV_RC, c0:c0 + CONV_CC])
                acc = acc + rows * cw_ref[w:w + 1, c0:c0 + CONV_CC]
            c_ref[r0:r0 + CONV_RC, c0:c0 + CONV_CC] = acc
    buf_ref[0:CTX_ROWS, :] = buf_ref[tt:tt + CTX_ROWS, :]
    y = _layer_norm(c_ref[...], g_ref[...], b_ref[...])
    y = y * jax.nn.sigmoid(y)
    o_ref[0] = jnp.dot(y.astype(BF16), w_ref[...], preferred_element_type=F32)


def _conv_branch(u, ctx, cw, cb, ln_g, ln_b, w_b, tt):
    bsz, t, d = u.shape
    nctx = ctx.shape[0]
    ctx_map = (lambda b, i: (b, 0, 0)) if nctx > 1 else (lambda b, i: (0, 0, 0))
    return pl.pallas_call(
        functools.partial(_conv_kernel, tt=tt, d=d),
        grid=(bsz, t // tt),
        in_specs=[pl.BlockSpec((1, tt, d), lambda b, i: (b, i, 0)),
                  pl.BlockSpec((1, CTX_ROWS, d), ctx_map),
                  _const_spec((CTX_ROWS, d)), _const_spec((1, d)), _const_spec((1, d)), _const_spec((1, d)),
                  _const_spec((d, d))],
        out_specs=pl.BlockSpec((1, tt, d), lambda b, i: (b, i, 0)),
        out_shape=jax.ShapeDtypeStruct((bsz, t, d), F32),
        scratch_shapes=[pltpu.VMEM((CTX_ROWS + tt, d), F32),
                        pltpu.VMEM((SUBLANES - 1, CTX_ROWS + tt - SUBLANES, d), F32),
                        pltpu.VMEM((tt, d), F32)],
        compiler_params=_params("parallel", "arbitrary"),
        name="conv_branch",
    )(u, ctx, cw, cb, ln_g, ln_b, w_b)


ATTN_RC = 32


def _softmax_step(carry, s, v):
    m_old, l_old, acc_old = carry
    m_new = jnp.maximum(m_old, jnp.max(s, axis=-1, keepdims=True))
    alpha = jnp.exp(m_old - m_new)
    p = jnp.exp(s - m_new)
    l_new = alpha * l_old + jnp.sum(p, axis=-1, keepdims=True)
    acc_new = alpha * acc_old + jnp.dot(p.astype(BF16), v, preferred_element_type=F32)
    return m_new, l_new, acc_new


def _scores(q, k):
    return lax.dot_general(q, k, (((1,), (1,)), ((), ())), preferred_element_type=F32)


def _diff_out(o1, o2, lam, g, out_scale):
    o = o1 - lam * o2
    o = o * lax.rsqrt(jnp.mean(o * o, axis=-1, keepdims=True) + LN_EPS)
    return o * g * out_scale


def _prompt_attn_kernel(sc_ref, q_ref, k_ref, v_ref, km_ref, vm_ref, bias_ref, bm_ref, g_ref, o_ref,
                        m_scr, acc_scr, *, tq, out_scale):
    qi = pl.program_id(2)
    lam = sc_ref[0]
    far = sc_ref[1 + pl.program_id(1)]
    vm = vm_ref[...]
    halves = [(m * HEAD_DIM, (m + 1) * HEAD_DIM) for m in range(2)]
    n_far = jnp.maximum(qi - 1, 0)
    q = q_ref[0]
    qs = [q[:, lo:hi] for lo, hi in halves]

    def with_ones(v):
        return jnp.concatenate([v, jnp.ones((v.shape[0], LANES), BF16)], axis=1)

    def lane_tiles(x, n):
        return jnp.concatenate([x] * n, axis=1)

    def row_max(s):
        return jnp.broadcast_to(jnp.max(s, axis=-1, keepdims=True), (s.shape[0], LANES))

    for m, (qm, (lo, hi)) in enumerate(zip(qs, halves)):
        s = _scores(qm, km_ref[:, lo:hi]) + bm_ref[0, jnp.minimum(qi, 1)]
        m0 = row_max(s)
        p = jnp.exp(s - m0[:, 0:N_META])
        m_scr[m] = m0
        acc_scr[m] = jnp.dot(p.astype(BF16), with_ones(vm), preferred_element_type=F32)

    def block(j, near):
        start = pl.multiple_of(j * tq, tq)
        vb = with_ones(v_ref[0, pl.ds(start, tq), :])
        shift = 0.0 if near else far
        ss, mbs = [], []
        for m, (qm, (lo, hi)) in enumerate(zip(qs, halves)):
            s = _scores(qm, k_ref[0, pl.ds(start, tq), lo:hi])
            if near:
                s = s + bias_ref[0, j - qi + 1]
            m_old = m_scr[m]
            m_new = jnp.maximum(m_old, row_max(s) + shift)
            m_scr[m] = m_new
            acc_scr[m] = lane_tiles(jnp.exp(m_old - m_new), 2) * acc_scr[m]
            ss.append(s)
            mbs.append(lane_tiles(m_new - shift, tq // LANES))
        for m in range(2):
            ps = []
            for c in range(tq // ATTN_RC):
                rs = slice(c * ATTN_RC, (c + 1) * ATTN_RC)
                ps.append(jnp.exp(ss[m][rs, :] - mbs[m][rs, :]).astype(BF16))
            acc_scr[m] += jnp.dot(jnp.concatenate(ps, axis=0), vb, preferred_element_type=F32)

    def far_body(j, _):
        block(j, False)
        return 0

    def near_body(j, _):
        block(j, True)
        return 0

    lax.fori_loop(0, n_far, far_body, 0)
    lax.fori_loop(n_far, qi + 1, near_body, 0)
    outs = [acc_scr[m, :, 0:V_DIM] / acc_scr[m, :, V_DIM:2 * V_DIM] for m in range(2)]
    o_ref[0] = _diff_out(outs[0], outs[1], lam, g_ref[...], out_scale)


def _prompt_attention(scal, q, kb, vb, km, vm, bias3, bm2, subln_g, tq, out_scale):
    bsz, t, d = q.shape
    return pl.pallas_call(
        functools.partial(_prompt_attn_kernel, tq=tq, out_scale=out_scale),
        grid=(bsz, N_HEADS, t // tq),
        in_specs=[pl.BlockSpec(memory_space=pltpu.SMEM),
                  pl.BlockSpec((1, tq, V_DIM), lambda b, h, i: (b, i, h)),
                  pl.BlockSpec((1, t, V_DIM), lambda b, h, i: (b, 0, h)),
                  pl.BlockSpec((1, t, V_DIM), lambda b, h, i: (b, 0, h)),
                  pl.BlockSpec((N_META, V_DIM), lambda b, h, i: (0, h)),
                  pl.BlockSpec((N_META, V_DIM), lambda b, h, i: (0, h)),
                  pl.BlockSpec((1, 2, tq, tq), lambda b, h, i: (h, 0, 0, 0)),
                  pl.BlockSpec((1, 2, tq, N_META), lambda b, h, i: (h, 0, 0, 0)),
                  _const_spec((1, V_DIM))],
        out_specs=pl.BlockSpec((1, tq, V_DIM), lambda b, h, i: (b, i, h)),
        out_shape=jax.ShapeDtypeStruct((bsz, t, d), F32),
        scratch_shapes=[pltpu.VMEM((2, tq, LANES), F32), pltpu.VMEM((2, tq, V_DIM + LANES), F32)],
        compiler_params=_params("parallel", "parallel", "arbitrary"),
        name="prompt_attn",
    )(scal, q, kb, vb, km, vm, bias3, bm2, subln_g)


def _sample_attn_kernel(sc_ref, q_ref, kc_ref, vc_ref, kn_ref, vn_ref, bc_ref, bn_ref, g_ref, o_ref,
                        *, out_scale):
    lam = sc_ref[0]
    q = q_ref[0]
    kc = kc_ref[0].astype(BF16)
    vc = vc_ref[0].astype(BF16)
    vn = vn_ref[0]
    outs = []
    for m in range(2):
        lo, hi = m * HEAD_DIM, (m + 1) * HEAD_DIM
        qm = q[:, lo:hi]
        s_c = _scores(qm, kc[:, lo:hi]) + bc_ref[0]
        s_n = _scores(qm, kn_ref[0, :, lo:hi]) + bn_ref[0]
        mx = jnp.maximum(jnp.max(s_c, axis=-1, keepdims=True), jnp.max(s_n, axis=-1, keepdims=True))
        p_c = jnp.exp(s_c - mx)
        p_n = jnp.exp(s_n - mx)
        l = jnp.sum(p_c, axis=-1, keepdims=True) + jnp.sum(p_n, axis=-1, keepdims=True)
        acc = (jnp.dot(p_c.astype(BF16), vc, preferred_element_type=F32)
               + jnp.dot(p_n.astype(BF16), vn, preferred_element_type=F32))
        outs.append(acc / l)
    o_ref[0] = _diff_out(outs[0], outs[1], lam, g_ref[...], out_scale)


def _sample_attention(scal, q, kc, vc, kn, vn, bias_c, bias_n, subln_g, out_scale):
    bsz, ts, d = q.shape
    past = kc.shape[1]
    per_bh = lambda n: pl.BlockSpec((1, n, V_DIM), lambda b, h: (b, 0, h))
    return pl.pallas_call(
        functools.partial(_sample_attn_kernel, out_scale=out_scale),
        grid=(bsz, N_HEADS),
        in_specs=[pl.BlockSpec(memory_space=pltpu.SMEM),
                  per_bh(ts), per_bh(past), per_bh(past), per_bh(ts), per_bh(ts),
                  pl.BlockSpec((1, ts, past), lambda b, h: (h, 0, 0)),
                  pl.BlockSpec((1, ts, ts), lambda b, h: (h, 0, 0)),
                  _const_spec((1, V_DIM))],
        out_specs=per_bh(ts),
        out_shape=jax.ShapeDtypeStruct((bsz, ts, d), F32),
        compiler_params=_params("parallel", "parallel"),
        name="sample_attn",
    )(scal, q, kc, vc, kn, vn, bias_c, bias_n, subln_g)


def _merge_kernel(h_ref, c_ref, a_ref, gs_ref, w_ref, g_ref, b_ref, o_ref, ob_ref, *, d):
    mix = gs_ref[:, 0:d] * c_ref[...] + gs_ref[:, d:2 * d] * a_ref[...]
    y = jnp.dot(mix.astype(BF16), w_ref[...], preferred_element_type=F32)
    h1 = _layer_norm(ALPHA * h_ref[...] + y, g_ref[...], b_ref[...])
    o_ref[...] = h1
    ob_ref[...] = h1.astype(BF16)


def _merge(h, conv_out, attn_out, gs, w_b, ln_g, ln_b, tm):
    r, d = h.shape
    row = lambda n: pl.BlockSpec((tm, n), lambda i: (i, 0))
    return pl.pallas_call(
        functools.partial(_merge_kernel, d=d),
        grid=(r // tm,),
        in_specs=[row(d), row(d), row(d), row(2 * d), _const_spec((d, d)), _const_spec((1, d)),
                  _const_spec((1, d))],
        out_specs=(row(d), row(d)),
        out_shape=(jax.ShapeDtypeStruct((r, d), F32), jax.ShapeDtypeStruct((r, d), BF16)),
        compiler_params=_params("parallel"),
        name="merge_ln1",
    )(h, conv_out, attn_out, gs, w_b, ln_g, ln_b)


def _top16_sorted(s, rowid16):
    work = s
    top = jnp.full(rowid16.shape, -jnp.inf, F32)
    for r in range(PEER_TOPK):
        mx = jnp.max(work, axis=0, keepdims=True)
        top = jnp.where(rowid16 == r, mx, top)
        if r + 1 < PEER_TOPK:
            work = jnp.where(work == mx, -jnp.inf, work)
    return top


def _peer_prep_kernel(x_ref, wq_ref, sk_ref, p_ref, t_ref, q_scr, *, tb):
    q_scr[...] = _scores(wq_ref[...], x_ref[...]).astype(BF16)
    rowid16 = lax.broadcasted_iota(jnp.int32, (PEER_TOPK, tb), 0)
    rowid8 = lax.broadcasted_iota(jnp.int32, (SUBLANES, tb), 0)

    def head(h, _):
        base = pl.multiple_of(h * 2 * PEER_HALF, 2 * PEER_HALF)
        s0 = jnp.dot(sk_ref[0], q_scr[pl.ds(base, PEER_HALF), :], preferred_element_type=F32)
        s1 = jnp.dot(sk_ref[1], q_scr[pl.ds(base + PEER_HALF, PEER_HALF), :], preferred_element_type=F32)
        a = _top16_sorted(s0, rowid16)
        b = _top16_sorted(s1, rowid16)
        cands = [a[0:1] + b[0:8], a[0:1] + b[8:16]]
        for i in range(1, 8):
            cands.append(jnp.where(rowid8 < PEER_TOPK // (i + 1), a[i:i + 1] + b[0:8], -jnp.inf))
        cands.append(a[8:16] + b[0:1])
        top = _top16_sorted(jnp.concatenate(cands, axis=0), rowid16)
        z = jnp.sum(jnp.exp(top - top[0:1]), axis=0, keepdims=True)
        p_ref[h, 0] = s0
        p_ref[h, 1] = jnp.exp(s0 - a[0:1]) / z
        p_ref[h, 2] = s1
        p_ref[h, 3] = jnp.exp(s1 - b[0:1])
        t_ref[h] = jnp.broadcast_to(top[PEER_TOPK - 1:PEER_TOPK], (SUBLANES, tb))
        return 0

    lax.fori_loop(0, PEER_HEADS, head, 0, unroll=2)


def _peer_prep(x, wq_t, sub_keys_b, tb):
    r, d = x.shape
    nq = wq_t.shape[0]
    return pl.pallas_call(
        functools.partial(_peer_prep_kernel, tb=tb),
        grid=(r // tb,),
        in_specs=[pl.BlockSpec((tb, d), lambda i: (i, 0)), _const_spec((nq, d)),
                  _const_spec((2, N_KEYS, PEER_HALF))],
        out_specs=(pl.BlockSpec((PEER_HEADS, 4, N_KEYS, tb), lambda i: (0, 0, 0, i)),
                   pl.BlockSpec((PEER_HEADS, SUBLANES, tb), lambda i: (0, 0, i))),
        out_shape=(jax.ShapeDtypeStruct((PEER_HEADS, 4, N_KEYS, r), F32),
                   jax.ShapeDtypeStruct((PEER_HEADS, SUBLANES, r), F32)),
        scratch_shapes=[pltpu.VMEM((nq, tb), BF16)],
        compiler_params=_params("parallel"),
        name="peer_prep",
    )(x, wq_t, sub_keys_b)


PEER_JT = 64
PEER_SUB = 256


def _peer_kernel(x_ref, xb_ref, p_ref, t_ref, u_ref, vt_ref, g_ref, b_ref, o_ref, acc_ref, *, ec, tb):
    e = pl.program_id(1)

    @pl.when(e == 0)
    def _():
        acc_ref[...] = jnp.zeros_like(acc_ref)

    n_g = PEER_SUB // N_KEYS
    n_sub = ec // PEER_SUB
    d = u_ref.shape[1]
    half = tb // 2
    n_k = d // MXU_TILE

    def pre_piece(sc, p):
        n, k = divmod(p, n_k)
        ks = slice(k * MXU_TILE, (k + 1) * MXU_TILE)
        return n, _scores(u_ref[sc * PEER_SUB:(sc + 1) * PEER_SUB, ks], xb_ref[n * half:(n + 1) * half, ks])

    def out_piece(sc, n, z):
        ns = slice(n * half, (n + 1) * half)
        acc_ref[:, ns] += jnp.dot(vt_ref[:, sc * PEER_SUB:(sc + 1) * PEER_SUB], z[:, ns],
                                  preferred_element_type=F32)

    def add_piece(pre, n, piece):
        pre[n] = piece if pre[n] is None else pre[n] + piece

    tiles = [(lc, jt) for lc in range(tb // LANES) for jt in range(N_KEYS // PEER_JT)]
    assert len(tiles) == 2 * n_k and tb // LANES == 4
    pre_next = [None, None]
    for p in range(2 * n_k):
        add_piece(pre_next, *pre_piece(0, p))
    z_prev = None
    for sc in range(n_sub):
        pre_cur, pre_next = pre_next, [None, None]
        i0 = (e * n_sub + sc) * n_g
        s0_rows = [[p_ref[h, 0, pl.ds(i0 + gi, 1), :] for h in range(PEER_HEADS)] for gi in range(n_g)]
        ea_rows = [[p_ref[h, 1, pl.ds(i0 + gi, 1), :] for h in range(PEER_HEADS)] for gi in range(n_g)]
        z_tiles = [[[None] * (tb // LANES) for _ in range(N_KEYS // PEER_JT)] for _ in range(n_g)]
        for it, (lc, jt) in enumerate(tiles):
            if sc + 1 < n_sub:
                add_piece(pre_next, *pre_piece(sc + 1, it))
            if z_prev is not None and it % n_k == 0:
                out_piece(sc - 1, it // n_k, z_prev)
            lanes = slice(lc * LANES, (lc + 1) * LANES)
            js = slice(jt * PEER_JT, (jt + 1) * PEER_JT)
            w = [jnp.zeros((PEER_JT, LANES), F32) for _ in range(n_g)]
            for h in range(PEER_HEADS):
                s1 = p_ref[h, 2, js, lanes]
                eb = p_ref[h, 3, js, lanes]
                thr = t_ref[h, 0:1, lanes]
                for gi in range(n_g):
                    sel = jnp.where(s0_rows[gi][h][:, lanes] + s1 >= thr, eb, 0.0)
                    w[gi] = w[gi] + sel * ea_rows[gi][h][:, lanes]
            pre_half = pre_cur[lc // 2]
            hl = slice((lc % 2) * LANES, (lc % 2 + 1) * LANES)
            for gi in range(n_g):
                pre = pre_half[gi * N_KEYS + jt * PEER_JT:gi * N_KEYS + (jt + 1) * PEER_JT, hl]
                act = 0.5 * pre * (1.0 + lax.erf(pre * (2.0 ** -0.5)))
                z_tiles[gi][jt][lc] = (w[gi] * act).astype(BF16)
        z_prev = jnp.concatenate([jnp.concatenate(t, axis=1) for g in z_tiles for t in g], axis=0)
    for n in range(2):
        out_piece(n_sub - 1, n, z_prev)

    @pl.when(e == pl.num_programs(1) - 1)
    def _():
        y = acc_ref[...].T
        o_ref[...] = _layer_norm(ALPHA * x_ref[...] + y, g_ref[...], b_ref[...])


def _peer(x, xb, pfac, thr, u_b, vt_b, ln_g, ln_b, tb, ec):
    r, d = x.shape
    n_exp = u_b.shape[0]
    return pl.pallas_call(
        functools.partial(_peer_kernel, ec=ec, tb=tb),
        grid=(r // tb, n_exp // ec),
        in_specs=[pl.BlockSpec((tb, d), lambda i, e: (i, 0)),
                  pl.BlockSpec((tb, d), lambda i, e: (i, 0)),
                  pl.BlockSpec((PEER_HEADS, 4, N_KEYS, tb), lambda i, e: (0, 0, 0, i)),
                  pl.BlockSpec((PEER_HEADS, SUBLANES, tb), lambda i, e: (0, 0, i)),
                  pl.BlockSpec((ec, d), lambda i, e: (e, 0)),
                  pl.BlockSpec((d, ec), lambda i, e: (0, e)),
                  _const_spec((1, d)), _const_spec((1, d))],
        out_specs=pl.BlockSpec((tb, d), lambda i, e: (i, 0)),
        out_shape=jax.ShapeDtypeStruct((r, d), F32),
        scratch_shapes=[pltpu.VMEM((d, tb), F32)],
        compiler_params=_params("parallel", "arbitrary"),
        name="peer_experts",
    )(x, xb, pfac, thr, u_b, vt_b, ln_g, ln_b)


def _t5_bucket(rel):
    half = NUM_BUCKETS // 2
    max_exact = half // 2
    n = jnp.abs(rel)
    large = max_exact + (jnp.log(jnp.maximum(n, 1).astype(F32) / max_exact)
                         / math.log(MAX_DISTANCE / max_exact) * (half - max_exact)).astype(jnp.int32)
    large = jnp.minimum(large, half - 1)
    return jnp.where(rel > 0, half, 0) + jnp.where(n < max_exact, n, large)


def _rel_bias_tile(rel_bias, rel):
    bucket = _t5_bucket(rel)[None]
    tile = jnp.zeros((rel_bias.shape[1],) + rel.shape, F32)
    for b in range(NUM_BUCKETS):
        tile = jnp.where(bucket == b, rel_bias[b].astype(F32)[:, None, None], tile)
    return tile


def _pick(n, prefs):
    for p in prefs:
        if n % p == 0:
            return p
    raise ValueError(f"no supported tile for extent {n}")


def kernel(x_prompt, x_sample, cache_k, cache_v, state_conv, meta_tokens, ln_in_g, ln_in_b, rel_bias, w_in, b_gate, conv_w, conv_b, conv_ln_g, conv_ln_b, w_conv_out, lam_q1, lam_k1, lam_q2, lam_k2, subln_g, w_out, ln1_g, ln1_b, w_peer_q, sub_keys, peer_u, peer_v, ln2_g, ln2_b):
    bsz, t, d = x_prompt.shape
    bd, ts, _ = x_sample.shape
    past = cache_k.shape[2]
    assert w_in.shape[0] == DEPTH == 1
    assert t >= CONV_WIDTH - 1 and ts >= CONV_WIDTH - 1 and ts % CTX_ROWS == 0
    l = 0
    row2 = lambda a: a.reshape(1, -1)

    lam_init = 0.8 - 0.6 * math.exp(-0.3 * l)
    lam = (jnp.exp(jnp.sum(lam_q1[l] * lam_k1[l]).astype(F32))
           - jnp.exp(jnp.sum(lam_q2[l] * lam_k2[l]).astype(F32)) + lam_init)
    scal = jnp.reshape(lam, (1,)).astype(F32)
    out_scale = 1.0 - lam_init

    w_in_b = w_in[l].astype(BF16)
    ln_g, ln_b = row2(ln_in_g), row2(ln_in_b)
    bg = row2(b_gate[l])

    tm = _pick(bsz * t, (256, 128, 64, 32, 16, 8))
    n_s = bd * ts
    tm_s = 256
    pad = (-(n_s + N_META)) % tm_s
    x_small = jnp.concatenate([x_sample.reshape(n_s, d), meta_tokens.astype(F32), jnp.zeros((pad, d), F32)], axis=0)
    hp, up, qp, kp, vp, kpb, vpb, gsp = _inproj(x_prompt.reshape(bsz * t, d), ln_g, ln_b, w_in_b, bg, tm)
    hs, us, qs, ks, vs, ksb, vsb, gss = _inproj(x_small, ln_g, ln_b, w_in_b, bg, tm_s)
    meta = slice(n_s, n_s + N_META)
    as3 = lambda a, b_: a.reshape(b_, -1, a.shape[-1])

    cw = jnp.concatenate([conv_w[l], jnp.zeros((CTX_ROWS - CONV_WIDTH, d), F32)], axis=0)
    conv_args = (cw, row2(conv_b[l]), row2(conv_ln_g[l]), row2(conv_ln_b[l]), w_conv_out[l].astype(BF16))
    ctx_p = jnp.concatenate([jnp.zeros((CTX_ROWS - N_META, d), F32), us[meta]], axis=0)[None]
    ctx_s = jnp.concatenate([jnp.zeros((bd, CTX_PAD, d), F32), state_conv[l]], axis=1)
    up3 = as3(up, bsz)
    us3 = as3(us[:n_s], bd)
    conv_p = _conv_branch(up3, ctx_p, *conv_args, tt=_pick(t, (256, 128, 64, 32)))
    conv_s = _conv_branch(us3, ctx_s, *conv_args, tt=ts)

    tq = _pick(t, (512, 256, 128, 64))
    assert tq % CHUNK == 0 and tq + 1 >= MAX_DISTANCE
    fq = jnp.arange(tq)
    rel_own = fq[None, :] - fq[:, None]
    own = _rel_bias_tile(rel_bias, rel_own) + jnp.where(fq[None, :] // CHUNK <= fq[:, None] // CHUNK, 0.0, NEG_INF)
    prev = _rel_bias_tile(rel_bias, rel_own - tq)
    far = _rel_bias_tile(rel_bias, jnp.full((1, 1), -(tq + 1)))
    bias3 = jnp.stack([prev, own], axis=1)
    scal = jnp.concatenate([scal, far.reshape(N_HEADS)])
    rel_meta = jnp.arange(N_META)[None, :] - (N_META + fq[:, None])
    bm = _rel_bias_tile(rel_bias, rel_meta)
    bm2 = jnp.stack([bm, jnp.broadcast_to(far, bm.shape)], axis=1)
    sg = row2(subln_g[l])
    attn_p = _prompt_attention(scal, as3(qp, bsz), as3(kpb, bsz), as3(vpb, bsz), ksb[meta], vsb[meta],
                               bias3, bm2, sg, tq, out_scale)

    k_pos = jnp.arange(past + ts)
    rel_s = k_pos[None, :] - k_pos[past:, None]
    bias_s = _rel_bias_tile(rel_bias, rel_s)
    attn_s = _sample_attention(scal, as3(qs[:n_s], bd), cache_k[l].reshape(bd, past, d),
                               cache_v[l].reshape(bd, past, d), as3(ksb[:n_s], bd), as3(vsb[:n_s], bd),
                               bias_s[:, :, :past], bias_s[:, :, past:], sg, out_scale)

    w_out_b = w_out[l].astype(BF16)
    h1p = _merge(hp, conv_p.reshape(-1, d), attn_p.reshape(-1, d), gsp, w_out_b, row2(ln1_g[l]), row2(ln1_b[l]), tm)
    h1s = _merge(hs[:n_s], conv_s.reshape(-1, d), attn_s.reshape(-1, d), gss[:n_s], w_out_b,
                 row2(ln1_g[l]), row2(ln1_b[l]), _pick(n_s, (256, 128, 64, 32, 16, 8)))

    wq_t = w_peer_q[l].T.astype(BF16)
    sk_b = sub_keys[l].astype(BF16)
    u_b = peer_u[l].astype(BF16)
    vt_b = peer_v[l].T.astype(BF16)

    def peer_ln2(h1, h1b):
        tb = _pick(h1.shape[0], (512, 256, 128))
        pfac, thr = _peer_prep(h1b, wq_t, sk_b, _pick(h1.shape[0], (256, 128)))
        return _peer(h1, h1b, pfac, thr, u_b, vt_b, row2(ln2_g[l]), row2(ln2_b[l]), tb, ec=1024)

    y_p = peer_ln2(*h1p).reshape(bsz, t, d)
    y_s = peer_ln2(*h1s).reshape(bd, ts, d)

    heads = lambda a, b_: a.reshape(1, b_, -1, N_HEADS, V_DIM)
    with_meta = lambda rows, m: jnp.concatenate([jnp.broadcast_to(m[None], (bsz, N_META, d)), as3(rows, bsz)], axis=1)
    keep = CONV_WIDTH - 1
    return (y_p, y_s,
            heads(with_meta(kp, ks[meta]), bsz), heads(with_meta(vp, vs[meta]), bsz), up3[None, :, t - keep:],
            heads(ks[:n_s], bd), heads(vs[:n_s], bd), us3[None, :, ts - keep:])
```

```python
import functools
import math

import jax
import jax.numpy as jnp
from jax import lax
from jax.experimental import pallas as pl
from jax.experimental.pallas import tpu as pltpu

F32 = jnp.float32
BF16 = jnp.bfloat16

DEPTH = 1
CHUNK = 64
N_META = 16
CONV_WIDTH = 31
N_HEADS = 8
HEAD_DIM = 64
V_DIM = 2 * HEAD_DIM
NUM_BUCKETS = 32
MAX_DISTANCE = 128
PEER_HEADS = 8
PEER_TOPK = 16
N_KEYS = 128
PEER_HALF = 128
ALPHA = (2.0 * DEPTH) ** 0.25
LN_EPS = 1e-5
NEG_INF = -1e30

SUBLANES = 8
LANES = 128
MXU_TILE = 256
VMEM_LIMIT = 56 * 1024 * 1024

CTX_ROWS = 32
CTX_PAD = CTX_ROWS - (CONV_WIDTH - 1)


def _layer_norm(x, g, b):
    mu = jnp.mean(x, axis=-1, keepdims=True)
    xc = x - mu
    var = jnp.mean(xc * xc, axis=-1, keepdims=True)
    return xc * lax.rsqrt(var + LN_EPS) * g + b


def _params(*sem):
    return pltpu.CompilerParams(dimension_semantics=sem, vmem_limit_bytes=VMEM_LIMIT)


def _const_spec(shape):
    nd = len(shape)
    return pl.BlockSpec(shape, lambda *_: (0,) * nd)


def _inproj_kernel(x_ref, g_ref, b_ref, w_ref, bg_ref,
                   h_ref, u_ref, q_ref, k_ref, v_ref, kb_ref, vb_ref, gs_ref, *, d):
    h = _layer_norm(x_ref[...], g_ref[...], b_ref[...])
    h_ref[...] = h
    hb = h.astype(BF16)

    def proj(c0, n):
        return jnp.dot(hb, w_ref[:, c0:c0 + n], preferred_element_type=F32)

    u_ref[...] = proj(0, d) * jax.nn.sigmoid(proj(d, d))
    q_ref[...] = (proj(2 * d, d) * (HEAD_DIM ** -0.5)).astype(BF16)
    k = proj(3 * d, d)
    k_ref[...] = k
    kb_ref[...] = k.astype(BF16)
    v = proj(4 * d, d)
    v_ref[...] = v
    vb_ref[...] = v.astype(BF16)
    gs_ref[...] = jax.nn.sigmoid(proj(5 * d, 2 * d) + bg_ref[...])


def _inproj(x, ln_g, ln_b, w_in_b, b_gate, tm):
    r, d = x.shape
    d_in = w_in_b.shape[1]
    row = lambda n: pl.BlockSpec((tm, n), lambda i: (i, 0))
    out_shape = (
        jax.ShapeDtypeStruct((r, d), F32),
        jax.ShapeDtypeStruct((r, d), F32),
        jax.ShapeDtypeStruct((r, d), BF16),
        jax.ShapeDtypeStruct((r, d), F32),
        jax.ShapeDtypeStruct((r, d), F32),
        jax.ShapeDtypeStruct((r, d), BF16),
        jax.ShapeDtypeStruct((r, d), BF16),
        jax.ShapeDtypeStruct((r, 2 * d), F32),
    )
    return pl.pallas_call(
        functools.partial(_inproj_kernel, d=d),
        grid=(r // tm,),
        in_specs=[row(d), _const_spec((1, d)), _const_spec((1, d)),
                  pl.BlockSpec((d, d_in), lambda i: (0, 0), pipeline_mode=pl.Buffered(1)),
                  _const_spec((1, 2 * d))],
        out_specs=(row(d), row(d), row(d), row(d), row(d), row(d), row(d), row(2 * d)),
        out_shape=out_shape,
        compiler_params=_params("parallel"),
        name="in_proj",
    )(x, ln_g, ln_b, w_in_b, b_gate)


CONV_RC = 32
CONV_CC = 256


def _conv_kernel(u_ref, ctx_ref, cw_ref, cb_ref, g_ref, b_ref, w_ref, o_ref, buf_ref, sh_ref, c_ref, *, tt, d):
    @pl.when(pl.program_id(1) == 0)
    def _():
        buf_ref[0:CTX_ROWS, :] = ctx_ref[0]

    buf_ref[CTX_ROWS:CTX_ROWS + tt, :] = u_ref[0]
    n_sh = tt + CTX_ROWS - SUBLANES
    for r in range(1, SUBLANES):
        sh_ref[r - 1, 0:n_sh, :] = buf_ref[r:r + n_sh, :]
    for r0 in range(0, tt, CONV_RC):
        for c0 in range(0, d, CONV_CC):
            acc = jnp.broadcast_to(cb_ref[:, c0:c0 + CONV_CC], (CONV_RC, CONV_CC))
            for w in range(CONV_WIDTH):
                a, r = divmod(CTX_PAD + w, SUBLANES)
                s = r0 + a * SUBLANES
                rows = (buf_ref[s:s + CONV_RC, c0:c0 + CONV_CC] if r == 0
                        else sh_ref[r - 1, s:s + CONV_RC, c0:c0 + CONV_CC])
                acc = acc + rows * cw_ref[w:w + 1, c0:c0 + CONV_CC]
            c_ref[r0:r0 + CONV_RC, c0:c0 + CONV_CC] = acc
    buf_ref[0:CTX_ROWS, :] = buf_ref[tt:tt + CTX_ROWS, :]
    y = _layer_norm(c_ref[...], g_ref[...], b_ref[...])
    y = y * jax.nn.sigmoid(y)
    o_ref[0] = jnp.dot(y.astype(BF16), w_ref[...], preferred_element_type=F32)


def _conv_branch(u, ctx, cw, cb, ln_g, ln_b, w_b, tt):
    bsz, t, d = u.shape
    nctx = ctx.shape[0]
    ctx_map = (lambda b, i: (b, 0, 0)) if nctx > 1 else (lambda b, i: (0, 0, 0))
    return pl.pallas_call(
        functools.partial(_conv_kernel, tt=tt, d=d),
        grid=(bsz, t // tt),
        in_specs=[pl.BlockSpec((1, tt, d), lambda b, i: (b, i, 0)),
                  pl.BlockSpec((1, CTX_ROWS, d), ctx_map),
                  _const_spec((CTX_ROWS, d)), _const_spec((1, d)), _const_spec((1, d)), _const_spec((1, d)),
                  _const_spec((d, d))],
        out_specs=pl.BlockSpec((1, tt, d), lambda b, i: (b, i, 0)),
        out_shape=jax.ShapeDtypeStruct((bsz, t, d), F32),
        scratch_shapes=[pltpu.VMEM((CTX_ROWS + tt, d), F32),
                        pltpu.VMEM((SUBLANES - 1, CTX_ROWS + tt - SUBLANES, d), F32),
                        pltpu.VMEM((tt, d), F32)],
        compiler_params=_params("parallel", "arbitrary"),
        name="conv_branch",
    )(u, ctx, cw, cb, ln_g, ln_b, w_b)


ATTN_RC = 32


def _softmax_step(carry, s, v):
    m_old, l_old, acc_old = carry
    m_new = jnp.maximum(m_old, jnp.max(s, axis=-1, keepdims=True))
    alpha = jnp.exp(m_old - m_new)
    p = jnp.exp(s - m_new)
    l_new = alpha * l_old + jnp.sum(p, axis=-1, keepdims=True)
    acc_new = alpha * acc_old + jnp.dot(p.astype(BF16), v, preferred_element_type=F32)
    return m_new, l_new, acc_new


def _scores(q, k):
    return lax.dot_general(q, k, (((1,), (1,)), ((), ())), preferred_element_type=F32)


def _diff_out(o1, o2, lam, g, out_scale):
    o = o1 - lam * o2
    o = o * lax.rsqrt(jnp.mean(o * o, axis=-1, keepdims=True) + LN_EPS)
    return o * g * out_scale


def _prompt_attn_kernel(sc_ref, q_ref, k_ref, v_ref, km_ref, vm_ref, bias_ref, bm_ref, g_ref, o_ref,
                        m_scr, acc_scr, *, tq, out_scale):
    qi = pl.program_id(2)
    lam = sc_ref[0]
    far = sc_ref[1 + pl.program_id(1)]
    vm = vm_ref[...]
    halves = [(m * HEAD_DIM, (m + 1) * HEAD_DIM) for m in range(2)]
    n_far = jnp.maximum(qi - 1, 0)
    q = q_ref[0]
    qs = [q[:, lo:hi] for lo, hi in halves]

    def with_ones(v):
        return jnp.concatenate([v, jnp.ones((v.shape[0], LANES), BF16)], axis=1)

    def lane_tiles(x, n):
        return jnp.concatenate([x] * n, axis=1)

    def row_max(s):
        return jnp.broadcast_to(jnp.max(s, axis=-1, keepdims=True), (s.shape[0], LANES))

    for m, (qm, (lo, hi)) in enumerate(zip(qs, halves)):
        s = _scores(qm, km_ref[:, lo:hi]) + bm_ref[0, jnp.minimum(qi, 1)]
        m0 = row_max(s)
        p = jnp.exp(s - m0[:, 0:N_META])
        m_scr[m] = m0
        acc_scr[m] = jnp.dot(p.astype(BF16), with_ones(vm), preferred_element_type=F32)

    def qk(j, near):
        start = pl.multiple_of(j * tq, tq)
        ss = []
        for qm, (lo, hi) in zip(qs, halves):
            s = _scores(qm, k_ref[0, pl.ds(start, tq), lo:hi])
            if near:
                s = s + bias_ref[0, j - qi + 1]
            ss.append(s)
        return ss

    def update(j, ss, near):
        start = pl.multiple_of(j * tq, tq)
        vb = with_ones(v_ref[0, pl.ds(start, tq), :])
        shift = 0.0 if near else far
        mbs = []
        for m in range(2):
            m_old = m_scr[m]
            m_new = jnp.maximum(m_old, row_max(ss[m]) + shift)
            m_scr[m] = m_new
            acc_scr[m] = lane_tiles(jnp.exp(m_old - m_new), 2) * acc_scr[m]
            mbs.append(lane_tiles(m_new - shift, tq // LANES))
        for m in range(2):
            ps = []
            for c in range(tq // ATTN_RC):
                rs = slice(c * ATTN_RC, (c + 1) * ATTN_RC)
                ps.append(jnp.exp(ss[m][rs, :] - mbs[m][rs, :]).astype(BF16))
            acc_scr[m] += jnp.dot(jnp.concatenate(ps, axis=0), vb, preferred_element_type=F32)

    def far_pair(i, _):
        s_a, s_b = qk(2 * i, False), qk(2 * i + 1, False)
        update(2 * i, s_a, False)
        update(2 * i + 1, s_b, False)
        return 0

    def far_body(j, _):
        update(j, qk(j, False), False)
        return 0

    def near_body(j, _):
        update(j, qk(j, True), True)
        return 0

    lax.fori_loop(0, n_far // 2, far_pair, 0)
    lax.fori_loop(2 * (n_far // 2), n_far, far_body, 0)
    lax.fori_loop(n_far, qi + 1, near_body, 0)
    outs = [acc_scr[m, :, 0:V_DIM] / acc_scr[m, :, V_DIM:2 * V_DIM] for m in range(2)]
    o_ref[0] = _diff_out(outs[0], outs[1], lam, g_ref[...], out_scale)


def _prompt_attention(scal, q, kb, vb, km, vm, bias3, bm2, subln_g, tq, out_scale):
    bsz, t, d = q.shape
    return pl.pallas_call(
        functools.partial(_prompt_attn_kernel, tq=tq, out_scale=out_scale),
        grid=(bsz, N_HEADS, t // tq),
        in_specs=[pl.BlockSpec(memory_space=pltpu.SMEM),
                  pl.BlockSpec((1, tq, V_DIM), lambda b, h, i: (b, i, h)),
                  pl.BlockSpec((1, t, V_DIM), lambda b, h, i: (b, 0, h)),
                  pl.BlockSpec((1, t, V_DIM), lambda b, h, i: (b, 0, h)),
                  pl.BlockSpec((N_META, V_DIM), lambda b, h, i: (0, h)),
                  pl.BlockSpec((N_META, V_DIM), lambda b, h, i: (0, h)),
                  pl.BlockSpec((1, 2, tq, tq), lambda b, h, i: (h, 0, 0, 0)),
                  pl.BlockSpec((1, 2, tq, N_META), lambda b, h, i: (h, 0, 0, 0)),
                  _const_spec((1, V_DIM))],
        out_specs=pl.BlockSpec((1, tq, V_DIM), lambda b, h, i: (b, i, h)),
        out_shape=jax.ShapeDtypeStruct((bsz, t, d), F32),
        scratch_shapes=[pltpu.VMEM((2, tq, LANES), F32), pltpu.VMEM((2, tq, V_DIM + LANES), F32)],
        compiler_params=_params("parallel", "parallel", "arbitrary"),
        name="prompt_attn",
    )(scal, q, kb, vb, km, vm, bias3, bm2, subln_g)


def _sample_attn_kernel(sc_ref, q_ref, kc_ref, vc_ref, kn_ref, vn_ref, bc_ref, bn_ref, g_ref, o_ref,
                        *, out_scale):
    lam = sc_ref[0]
    q = q_ref[0]
    kc = kc_ref[0].astype(BF16)
    vc = vc_ref[0].astype(BF16)
    vn = vn_ref[0]
    outs = []
    for m in range(2):
        lo, hi = m * HEAD_DIM, (m + 1) * HEAD_DIM
        qm = q[:, lo:hi]
        s_c = _scores(qm, kc[:, lo:hi]) + bc_ref[0]
        s_n = _scores(qm, kn_ref[0, :, lo:hi]) + bn_ref[0]
        mx = jnp.maximum(jnp.max(s_c, axis=-1, keepdims=True), jnp.max(s_n, axis=-1, keepdims=True))
        p_c = jnp.exp(s_c - mx)
        p_n = jnp.exp(s_n - mx)
        l = jnp.sum(p_c, axis=-1, keepdims=True) + jnp.sum(p_n, axis=-1, keepdims=True)
        acc = (jnp.dot(p_c.astype(BF16), vc, preferred_element_type=F32)
               + jnp.dot(p_n.astype(BF16), vn, preferred_element_type=F32))
        outs.append(acc / l)
    o_ref[0] = _diff_out(outs[0], outs[1], lam, g_ref[...], out_scale)


def _sample_attention(scal, q, kc, vc, kn, vn, bias_c, bias_n, subln_g, out_scale):
    bsz, ts, d = q.shape
    past = kc.shape[1]
    per_bh = lambda n: pl.BlockSpec((1, n, V_DIM), lambda b, h: (b, 0, h))
    return pl.pallas_call(
        functools.partial(_sample_attn_kernel, out_scale=out_scale),
        grid=(bsz, N_HEADS),
        in_specs=[pl.BlockSpec(memory_space=pltpu.SMEM),
                  per_bh(ts), per_bh(past), per_bh(past), per_bh(ts), per_bh(ts),
                  pl.BlockSpec((1, ts, past), lambda b, h: (h, 0, 0)),
                  pl.BlockSpec((1, ts, ts), lambda b, h: (h, 0, 0)),
                  _const_spec((1, V_DIM))],
        out_specs=per_bh(ts),
        out_shape=jax.ShapeDtypeStruct((bsz, ts, d), F32),
        compiler_params=_params("parallel", "parallel"),
        name="sample_attn",
    )(scal, q, kc, vc, kn, vn, bias_c, bias_n, subln_g)


def _merge_kernel(h_ref, c_ref, a_ref, gs_ref, w_ref, g_ref, b_ref, o_ref, ob_ref, *, d):
    mix = gs_ref[:, 0:d] * c_ref[...] + gs_ref[:, d:2 * d] * a_ref[...]
    y = jnp.dot(mix.astype(BF16), w_ref[...], preferred_element_type=F32)
    h1 = _layer_norm(ALPHA * h_ref[...] + y, g_ref[...], b_ref[...])
    o_ref[...] = h1
    ob_ref[...] = h1.astype(BF16)


def _merge(h, conv_out, attn_out, gs, w_b, ln_g, ln_b, tm):
    r, d = h.shape
    row = lambda n: pl.BlockSpec((tm, n), lambda i: (i, 0))
    return pl.pallas_call(
        functools.partial(_merge_kernel, d=d),
        grid=(r // tm,),
        in_specs=[row(d), row(d), row(d), row(2 * d), _const_spec((d, d)), _const_spec((1, d)),
                  _const_spec((1, d))],
        out_specs=(row(d), row(d)),
        out_shape=(jax.ShapeDtypeStruct((r, d), F32), jax.ShapeDtypeStruct((r, d), BF16)),
        compiler_params=_params("parallel"),
        name="merge_ln1",
    )(h, conv_out, attn_out, gs, w_b, ln_g, ln_b)


def _top16_sorted(s, rowid16):
    work = s
    top = jnp.full(rowid16.shape, -jnp.inf, F32)
    for r in range(PEER_TOPK):
        mx = jnp.max(work, axis=0, keepdims=True)
        top = jnp.where(rowid16 == r, mx, top)
        if r + 1 < PEER_TOPK:
            work = jnp.where(work == mx, -jnp.inf, work)
    return top


def _peer_prep_kernel(x_ref, wq_ref, sk_ref, p_ref, t_ref, q_scr, *, tb):
    q_scr[...] = _scores(wq_ref[...], x_ref[...]).astype(BF16)
    rowid16 = lax.broadcasted_iota(jnp.int32, (PEER_TOPK, tb), 0)
    rowid8 = lax.broadcasted_iota(jnp.int32, (SUBLANES, tb), 0)

    def head(h, _):
        base = pl.multiple_of(h * 2 * PEER_HALF, 2 * PEER_HALF)
        s0 = jnp.dot(sk_ref[0], q_scr[pl.ds(base, PEER_HALF), :], preferred_element_type=F32)
        s1 = jnp.dot(sk_ref[1], q_scr[pl.ds(base + PEER_HALF, PEER_HALF), :], preferred_element_type=F32)
        a = _top16_sorted(s0, rowid16)
        b = _top16_sorted(s1, rowid16)
        cands = [a[0:1] + b[0:8], a[0:1] + b[8:16]]
        for i in range(1, 8):
            cands.append(jnp.where(rowid8 < PEER_TOPK // (i + 1), a[i:i + 1] + b[0:8], -jnp.inf))
        cands.append(a[8:16] + b[0:1])
        top = _top16_sorted(jnp.concatenate(cands, axis=0), rowid16)
        z = jnp.sum(jnp.exp(top - top[0:1]), axis=0, keepdims=True)
        p_ref[h, 0] = s0
        p_ref[h, 1] = jnp.exp(s0 - a[0:1]) / z
        p_ref[h, 2] = s1
        p_ref[h, 3] = jnp.exp(s1 - b[0:1])
        t_ref[h] = jnp.broadcast_to(top[PEER_TOPK - 1:PEER_TOPK], (SUBLANES, tb))
        return 0

    lax.fori_loop(0, PEER_HEADS, head, 0, unroll=2)


def _peer_prep(x, wq_t, sub_keys_b, tb):
    r, d = x.shape
    nq = wq_t.shape[0]
    return pl.pallas_call(
        functools.partial(_peer_prep_kernel, tb=tb),
        grid=(r // tb,),
        in_specs=[pl.BlockSpec((tb, d), lambda i: (i, 0)), _const_spec((nq, d)),
                  _const_spec((2, N_KEYS, PEER_HALF))],
        out_specs=(pl.BlockSpec((PEER_HEADS, 4, N_KEYS, tb), lambda i: (0, 0, 0, i)),
                   pl.BlockSpec((PEER_HEADS, SUBLANES, tb), lambda i: (0, 0, i))),
        out_shape=(jax.ShapeDtypeStruct((PEER_HEADS, 4, N_KEYS, r), F32),
                   jax.ShapeDtypeStruct((PEER_HEADS, SUBLANES, r), F32)),
        scratch_shapes=[pltpu.VMEM((nq, tb), BF16)],
        compiler_params=_params("parallel"),
        name="peer_prep",
    )(x, wq_t, sub_keys_b)


PEER_JT = 32
PEER_SUB = 256
PEER_ZR = 64


def _peer_kernel(unit_ref, x_ref, xb_ref, p_ref, t_ref, u_ref, vt_ref, g_ref, b_ref, o_ref,
                 acc_ref, pre_ref, w_ref, z_ref, vprev_ref, *, ec, tb):
    e = pl.program_id(1)

    @pl.when(e == 0)
    def _():
        acc_ref[...] = jnp.zeros_like(acc_ref)
        z_ref[...] = jnp.zeros_like(z_ref)
        vprev_ref[...] = jnp.zeros_like(vprev_ref)

    n_sub = ec // PEER_SUB
    d = u_ref.shape[1]
    half = tb // 2
    n_k = d // MXU_TILE

    def pre_piece(sc, p):
        n, k = divmod(p, n_k)
        ks = slice(k * MXU_TILE, (k + 1) * MXU_TILE)
        return n, _scores(u_ref[sc * PEER_SUB:(sc + 1) * PEER_SUB, ks], xb_ref[n * half:(n + 1) * half, ks])

    def out_piece(sp, n, v_ref):
        ns = slice(n * half, (n + 1) * half)
        es = slice(sp * 2 * PEER_SUB, (sp + 1) * 2 * PEER_SUB)
        acc_ref[:, ns] += jnp.dot(v_ref[:, es], z_ref[es, ns], preferred_element_type=F32)

    def add_piece(pre, n, piece):
        pre[n] = piece if pre[n] is None else pre[n] + piece

    n_grp = ec // N_KEYS
    i0 = e * n_grp
    s0_rows = [[p_ref[h, 0, pl.ds(i0 + gi, 1), :] for h in range(PEER_HEADS)] for gi in range(n_grp)]
    ea_rows = [[p_ref[h, 1, pl.ds(i0 + gi, 1), :] for h in range(PEER_HEADS)] for gi in range(n_grp)]
    tiles = [(lc, jt) for lc in range(tb // LANES) for jt in range(N_KEYS // PEER_JT)]
    pre_jobs = [(sc, p) for sc in range(n_sub) for p in range(2 * n_k)]
    out_jobs = [(sp, n) for sp in range(ec // (2 * PEER_SUB)) for n in range(2)]
    assert len(pre_jobs) % len(tiles) == 0 and len(tiles) % len(out_jobs) == 0
    pre_per_tile = len(pre_jobs) // len(tiles)
    tiles_per_out = len(tiles) // len(out_jobs)
    pre_acc = [None, None]
    for it, (lc, jt) in enumerate(tiles):
        for sc, p in pre_jobs[it * pre_per_tile:(it + 1) * pre_per_tile]:
            n, piece = pre_piece(sc, p)
            add_piece(pre_acc, n, piece)
            if p % n_k == n_k - 1:
                pre_ref[sc * PEER_SUB:(sc + 1) * PEER_SUB, n * half:(n + 1) * half] = pre_acc[n]
                pre_acc[n] = None
        if it % tiles_per_out == 0:
            sp, n = out_jobs[it // tiles_per_out]
            out_piece(sp, n, vprev_ref)
        lanes = slice(lc * LANES, (lc + 1) * LANES)
        js = slice(jt * PEER_JT, (jt + 1) * PEER_JT)
        w = [jnp.zeros((PEER_JT, LANES), F32) for _ in range(n_grp)]
        for h in range(PEER_HEADS):
            s1 = p_ref[h, 2, js, lanes] * unit_ref[0]
            eb = p_ref[h, 3, js, lanes] * unit_ref[0]
            thr = t_ref[h, 0:1, lanes]
            for gi in range(n_grp):
                sel = jnp.where(s0_rows[gi][h][:, lanes] + s1 >= thr, eb, 0.0)
                w[gi] = w[gi] + sel * ea_rows[gi][h][:, lanes]
        for gi in range(n_grp):
            w_ref[gi * N_KEYS + jt * PEER_JT:gi * N_KEYS + (jt + 1) * PEER_JT, lanes] = w[gi]

    for r0 in range(0, ec, PEER_ZR):
        pre = pre_ref[r0:r0 + PEER_ZR, :]
        act = 0.5 * pre * (1.0 + lax.erf(pre * (2.0 ** -0.5)))
        z_ref[r0:r0 + PEER_ZR, :] = (w_ref[r0:r0 + PEER_ZR, :] * act).astype(BF16)
    vprev_ref[...] = vt_ref[...]

    @pl.when(e == pl.num_programs(1) - 1)
    def _():
        for sp, n in out_jobs:
            out_piece(sp, n, vt_ref)
        y = acc_ref[...].T
        o_ref[...] = _layer_norm(ALPHA * x_ref[...] + y, g_ref[...], b_ref[...])


def _peer(x, xb, pfac, thr, u_b, vt_b, ln_g, ln_b, tb, ec):
    r, d = x.shape
    n_exp = u_b.shape[0]
    return pl.pallas_call(
        functools.partial(_peer_kernel, ec=ec, tb=tb),
        grid=(r // tb, n_exp // ec),
        in_specs=[pl.BlockSpec(memory_space=pltpu.SMEM),
                  pl.BlockSpec((tb, d), lambda i, e: (i, 0)),
                  pl.BlockSpec((tb, d), lambda i, e: (i, 0)),
                  pl.BlockSpec((PEER_HEADS, 4, N_KEYS, tb), lambda i, e: (0, 0, 0, i)),
                  pl.BlockSpec((PEER_HEADS, SUBLANES, tb), lambda i, e: (0, 0, i)),
                  pl.BlockSpec((ec, d), lambda i, e: (e, 0)),
                  pl.BlockSpec((d, ec), lambda i, e: (0, e)),
                  _const_spec((1, d)), _const_spec((1, d))],
        out_specs=pl.BlockSpec((tb, d), lambda i, e: (i, 0)),
        out_shape=jax.ShapeDtypeStruct((r, d), F32),
        scratch_shapes=[pltpu.VMEM((d, tb), F32), pltpu.VMEM((ec, tb), F32), pltpu.VMEM((ec, tb), F32),
                        pltpu.VMEM((ec, tb), BF16), pltpu.VMEM((d, ec), BF16)],
        compiler_params=_params("parallel", "arbitrary"),
        name="peer_experts",
    )(jnp.ones((1,), F32), x, xb, pfac, thr, u_b, vt_b, ln_g, ln_b)


def _t5_bucket(rel):
    half = NUM_BUCKETS // 2
    max_exact = half // 2
    n = jnp.abs(rel)
    large = max_exact + (jnp.log(jnp.maximum(n, 1).astype(F32) / max_exact)
                         / math.log(MAX_DISTANCE / max_exact) * (half - max_exact)).astype(jnp.int32)
    large = jnp.minimum(large, half - 1)
    return jnp.where(rel > 0, half, 0) + jnp.where(n < max_exact, n, large)


def _rel_bias_tile(rel_bias, rel):
    bucket = _t5_bucket(rel)[None]
    tile = jnp.zeros((rel_bias.shape[1],) + rel.shape, F32)
    for b in range(NUM_BUCKETS):
        tile = jnp.where(bucket == b, rel_bias[b].astype(F32)[:, None, None], tile)
    return tile


def _pick(n, prefs):
    for p in prefs:
        if n % p == 0:
            return p
    raise ValueError(f"no supported tile for extent {n}")


def kernel(x_prompt, x_sample, cache_k, cache_v, state_conv, meta_tokens, ln_in_g, ln_in_b, rel_bias, w_in, b_gate, conv_w, conv_b, conv_ln_g, conv_ln_b, w_conv_out, lam_q1, lam_k1, lam_q2, lam_k2, subln_g, w_out, ln1_g, ln1_b, w_peer_q, sub_keys, peer_u, peer_v, ln2_g, ln2_b):
    bsz, t, d = x_prompt.shape
    bd, ts, _ = x_sample.shape
    past = cache_k.shape[2]
    assert w_in.shape[0] == DEPTH == 1
    assert t >= CONV_WIDTH - 1 and ts >= CONV_WIDTH - 1 and ts % CTX_ROWS == 0
    l = 0
    row2 = lambda a: a.reshape(1, -1)

    lam_init = 0.8 - 0.6 * math.exp(-0.3 * l)
    lam = (jnp.exp(jnp.sum(lam_q1[l] * lam_k1[l]).astype(F32))
           - jnp.exp(jnp.sum(lam_q2[l] * lam_k2[l]).astype(F32)) + lam_init)
    scal = jnp.reshape(lam, (1,)).astype(F32)
    out_scale = 1.0 - lam_init

    w_in_b = w_in[l].astype(BF16)
    ln_g, ln_b = row2(ln_in_g), row2(ln_in_b)
    bg = row2(b_gate[l])

    tm = _pick(bsz * t, (256, 128, 64, 32, 16, 8))
    n_s = bd * ts
    tm_s = 256
    pad = (-(n_s + N_META)) % tm_s
    x_small = jnp.concatenate([x_sample.reshape(n_s, d), meta_tokens.astype(F32), jnp.zeros((pad, d), F32)], axis=0)
    hp, up, qp, kp, vp, kpb, vpb, gsp = _inproj(x_prompt.reshape(bsz * t, d), ln_g, ln_b, w_in_b, bg, tm)
    hs, us, qs, ks, vs, ksb, vsb, gss = _inproj(x_small, ln_g, ln_b, w_in_b, bg, tm_s)
    meta = slice(n_s, n_s + N_META)
    as3 = lambda a, b_: a.reshape(b_, -1, a.shape[-1])

    cw = jnp.concatenate([conv_w[l], jnp.zeros((CTX_ROWS - CONV_WIDTH, d), F32)], axis=0)
    conv_args = (cw, row2(conv_b[l]), row2(conv_ln_g[l]), row2(conv_ln_b[l]), w_conv_out[l].astype(BF16))
    ctx_p = jnp.concatenate([jnp.zeros((CTX_ROWS - N_META, d), F32), us[meta]], axis=0)[None]
    ctx_s = jnp.concatenate([jnp.zeros((bd, CTX_PAD, d), F32), state_conv[l]], axis=1)
    up3 = as3(up, bsz)
    us3 = as3(us[:n_s], bd)
    conv_p = _conv_branch(up3, ctx_p, *conv_args, tt=_pick(t, (256, 128, 64, 32)))
    conv_s = _conv_branch(us3, ctx_s, *conv_args, tt=ts)

    tq = _pick(t, (512, 256, 128, 64))
    assert tq % CHUNK == 0 and tq + 1 >= MAX_DISTANCE
    fq = jnp.arange(tq)
    rel_own = fq[None, :] - fq[:, None]
    own = _rel_bias_tile(rel_bias, rel_own) + jnp.where(fq[None, :] // CHUNK <= fq[:, None] // CHUNK, 0.0, NEG_INF)
    prev = _rel_bias_tile(rel_bias, rel_own - tq)
    far = _rel_bias_tile(rel_bias, jnp.full((1, 1), -(tq + 1)))
    bias3 = jnp.stack([prev, own], axis=1)
    scal = jnp.concatenate([scal, far.reshape(N_HEADS)])
    rel_meta = jnp.arange(N_META)[None, :] - (N_META + fq[:, None])
    bm = _rel_bias_tile(rel_bias, rel_meta)
    bm2 = jnp.stack([bm, jnp.broadcast_to(far, bm.shape)], axis=1)
    sg = row2(subln_g[l])
    attn_p = _prompt_attention(scal, as3(qp, bsz), as3(kpb, bsz), as3(vpb, bsz), ksb[meta], vsb[meta],
                               bias3, bm2, sg, tq, out_scale)

    k_pos = jnp.arange(past + ts)
    rel_s = k_pos[None, :] - k_pos[past:, None]
    bias_s = _rel_bias_tile(rel_bias, rel_s)
    attn_s = _sample_attention(scal, as3(qs[:n_s], bd), cache_k[l].reshape(bd, past, d),
                               cache_v[l].reshape(bd, past, d), as3(ksb[:n_s], bd), as3(vsb[:n_s], bd),
                               bias_s[:, :, :past], bias_s[:, :, past:], sg, out_scale)

    w_out_b = w_out[l].astype(BF16)
    h1p = _merge(hp, conv_p.reshape(-1, d), attn_p.reshape(-1, d), gsp, w_out_b, row2(ln1_g[l]), row2(ln1_b[l]), tm)
    h1s = _merge(hs[:n_s], conv_s.reshape(-1, d), attn_s.reshape(-1, d), gss[:n_s], w_out_b,
                 row2(ln1_g[l]), row2(ln1_b[l]), _pick(n_s, (256, 128, 64, 32, 16, 8)))

    wq_t = w_peer_q[l].T.astype(BF16)
    sk_b = sub_keys[l].astype(BF16)
    u_b = peer_u[l].astype(BF16)
    vt_b = peer_v[l].T.astype(BF16)

    def peer_ln2(h1, h1b):
        tb = _pick(h1.shape[0], (512, 256, 128))
        pfac, thr = _peer_prep(h1b, wq_t, sk_b, _pick(h1.shape[0], (256, 128)))
        return _peer(h1, h1b, pfac, thr, u_b, vt_b, row2(ln2_g[l]), row2(ln2_b[l]), tb, ec=1024)

    y_p = peer_ln2(*h1p).reshape(bsz, t, d)
    y_s = peer_ln2(*h1s).reshape(bd, ts, d)

    heads = lambda a, b_: a.reshape(1, b_, -1, N_HEADS, V_DIM)
    with_meta = lambda rows, m: jnp.concatenate([jnp.broadcast_to(m[None], (bsz, N_META, d)), as3(rows, bsz)], axis=1)
    keep = CONV_WIDTH - 1
    return (y_p, y_s,
            heads(with_meta(kp, ks[meta]), bsz), heads(with_meta(vp, vs[meta]), bsz), up3[None, :, t - keep:],
            heads(ks[:n_s], bd), heads(vs[:n_s], bd), us3[None, :, ts - keep:])
```

```python
import functools
import math

import jax
import jax.numpy as jnp
from jax import lax
from jax.experimental import pallas as pl
from jax.experimental.pallas import tpu as pltpu

F32 = jnp.float32
BF16 = jnp.bfloat16

DEPTH = 1
CHUNK = 64
N_META = 16
CONV_WIDTH = 31
N_HEADS = 8
HEAD_DIM = 64
V_DIM = 2 * HEAD_DIM
NUM_BUCKETS = 32
MAX_DISTANCE = 128
PEER_HEADS = 8
PEER_TOPK = 16
N_KEYS = 128
PEER_HALF = 128
ALPHA = (2.0 * DEPTH) ** 0.25
LN_EPS = 1e-5
NEG_INF = -1e30

SUBLANES = 8
LANES = 128
MXU_TILE = 256
VMEM_LIMIT = 56 * 1024 * 1024

CTX_ROWS = 32
CTX_PAD = CTX_ROWS - (CONV_WIDTH - 1)


def _layer_norm(x, g, b):
    mu = jnp.mean(x, axis=-1, keepdims=True)
    xc = x - mu
    var = jnp.mean(xc * xc, axis=-1, keepdims=True)
    return xc * lax.rsqrt(var + LN_EPS) * g + b


def _params(*sem):
    return pltpu.CompilerParams(dimension_semantics=sem, vmem_limit_bytes=VMEM_LIMIT)


def _const_spec(shape):
    nd = len(shape)
    return pl.BlockSpec(shape, lambda *_: (0,) * nd)


def _inproj_kernel(x_ref, g_ref, b_ref, w_ref, bg_ref,
                   h_ref, u_ref, q_ref, k_ref, v_ref, kb_ref, vb_ref, gs_ref, *, d):
    h = _layer_norm(x_ref[...], g_ref[...], b_ref[...])
    h_ref[...] = h
    hb = h.astype(BF16)

    def proj(c0, n):
        return jnp.dot(hb, w_ref[:, c0:c0 + n], preferred_element_type=F32)

    u_ref[...] = proj(0, d) * jax.nn.sigmoid(proj(d, d))
    q_ref[...] = (proj(2 * d, d) * (HEAD_DIM ** -0.5)).astype(BF16)
    k = proj(3 * d, d)
    k_ref[...] = k
    kb_ref[...] = k.astype(BF16)
    v = proj(4 * d, d)
    v_ref[...] = v
    vb_ref[...] = v.astype(BF16)
    gs_ref[...] = jax.nn.sigmoid(proj(5 * d, 2 * d) + bg_ref[...])


def _inproj(x, ln_g, ln_b, w_in_b, b_gate, tm):
    r, d = x.shape
    d_in = w_in_b.shape[1]
    row = lambda n: pl.BlockSpec((tm, n), lambda i: (i, 0))
    out_shape = (
        jax.ShapeDtypeStruct((r, d), F32),
        jax.ShapeDtypeStruct((r, d), F32),
        jax.ShapeDtypeStruct((r, d), BF16),
        jax.ShapeDtypeStruct((r, d), F32),
        jax.ShapeDtypeStruct((r, d), F32),
        jax.ShapeDtypeStruct((r, d), BF16),
        jax.ShapeDtypeStruct((r, d), BF16),
        jax.ShapeDtypeStruct((r, 2 * d), F32),
    )
    return pl.pallas_call(
        functools.partial(_inproj_kernel, d=d),
        grid=(r // tm,),
        in_specs=[row(d), _const_spec((1, d)), _const_spec((1, d)),
                  pl.BlockSpec((d, d_in), lambda i: (0, 0), pipeline_mode=pl.Buffered(1)),
                  _const_spec((1, 2 * d))],
        out_specs=(row(d), row(d), row(d), row(d), row(d), row(d), row(d), row(2 * d)),
        out_shape=out_shape,
        compiler_params=_params("parallel"),
        name="in_proj",
    )(x, ln_g, ln_b, w_in_b, b_gate)


CONV_RC = 32
CONV_CC = 256


def _conv_kernel(u_ref, ctx_ref, cw_ref, cb_ref, g_ref, b_ref, w_ref, o_ref, buf_ref, sh_ref, c_ref, *, tt, d):
    @pl.when(pl.program_id(1) == 0)
    def _():
        buf_ref[0:CTX_ROWS, :] = ctx_ref[0]

    buf_ref[CTX_ROWS:CTX_ROWS + tt, :] = u_ref[0]
    n_sh = tt + CTX_ROWS - SUBLANES
    for r in range(1, SUBLANES):
        sh_ref[r - 1, 0:n_sh, :] = buf_ref[r:r + n_sh, :]
    for r0 in range(0, tt, CONV_RC):
        for c0 in range(0, d, CONV_CC):
            acc = jnp.broadcast_to(cb_ref[:, c0:c0 + CONV_CC], (CONV_RC, CONV_CC))
            for w in range(CONV_WIDTH):
                a, r = divmod(CTX_PAD + w, SUBLANES)
                s = r0 + a * SUBLANES
                rows = (buf_ref[s:s + CONV_RC, c0:c0 + CONV_CC] if r == 0
                        else sh_ref[r - 1, s:s + CONV_RC, c0:c0 + CONV_CC])
                acc = acc + rows * cw_ref[w:w + 1, c0:c0 + CONV_CC]
            c_ref[r0:r0 + CONV_RC, c0:c0 + CONV_CC] = acc
    buf_ref[0:CTX_ROWS, :] = buf_ref[tt:tt + CTX_ROWS, :]
    y = _layer_norm(c_ref[...], g_ref[...], b_ref[...])
    y = y * jax.nn.sigmoid(y)
    o_ref[0] = jnp.dot(y.astype(BF16), w_ref[...], preferred_element_type=F32)


def _conv_branch(u, ctx, cw, cb, ln_g, ln_b, w_b, tt):
    bsz, t, d = u.shape
    nctx = ctx.shape[0]
    ctx_map = (lambda b, i: (b, 0, 0)) if nctx > 1 else (lambda b, i: (0, 0, 0))
    return pl.pallas_call(
        functools.partial(_conv_kernel, tt=tt, d=d),
        grid=(bsz, t // tt),
        in_specs=[pl.BlockSpec((1, tt, d), lambda b, i: (b, i, 0)),
                  pl.BlockSpec((1, CTX_ROWS, d), ctx_map),
                  _const_spec((CTX_ROWS, d)), _const_spec((1, d)), _const_spec((1, d)), _const_spec((1, d)),
                  _const_spec((d, d))],
        out_specs=pl.BlockSpec((1, tt, d), lambda b, i: (b, i, 0)),
        out_shape=jax.ShapeDtypeStruct((bsz, t, d), F32),
        scratch_shapes=[pltpu.VMEM((CTX_ROWS + tt, d), F32),
                        pltpu.VMEM((SUBLANES - 1, CTX_ROWS + tt - SUBLANES, d), F32),
                        pltpu.VMEM((tt, d), F32)],
        compiler_params=_params("parallel", "arbitrary"),
        name="conv_branch",
    )(u, ctx, cw, cb, ln_g, ln_b, w_b)


ATTN_RC = 32
ATTN_GROUPS = (4, 2, 1)


def _softmax_step(carry, s, v):
    m_old, l_old, acc_old = carry
    m_new = jnp.maximum(m_old, jnp.max(s, axis=-1, keepdims=True))
    alpha = jnp.exp(m_old - m_new)
    p = jnp.exp(s - m_new)
    l_new = alpha * l_old + jnp.sum(p, axis=-1, keepdims=True)
    acc_new = alpha * acc_old + jnp.dot(p.astype(BF16), v, preferred_element_type=F32)
    return m_new, l_new, acc_new


def _scores(q, k):
    return lax.dot_general(q, k, (((1,), (1,)), ((), ())), preferred_element_type=F32)


def _diff_out(o1, o2, lam, g, out_scale):
    o = o1 - lam * o2
    o = o * lax.rsqrt(jnp.mean(o * o, axis=-1, keepdims=True) + LN_EPS)
    return o * g * out_scale


def _prompt_attn_kernel(sc_ref, q_ref, k_ref, v_ref, km_ref, vm_ref, bias_ref, bm_ref, g_ref, o_ref,
                        m_scr, acc_scr, *, tq, out_scale):
    qi = pl.program_id(2)
    lam = sc_ref[0]
    far = sc_ref[1 + pl.program_id(1)]
    vm = vm_ref[...]
    halves = [(m * HEAD_DIM, (m + 1) * HEAD_DIM) for m in range(2)]
    n_far = jnp.maximum(qi - 1, 0)
    q = q_ref[0]
    qs = [q[:, lo:hi] for lo, hi in halves]

    def with_ones(v):
        return jnp.concatenate([v, jnp.ones((v.shape[0], LANES), BF16)], axis=1)

    def lane_tiles(x, n):
        return jnp.concatenate([x] * n, axis=1)

    def row_max(s):
        return jnp.broadcast_to(jnp.max(s, axis=-1, keepdims=True), (s.shape[0], LANES))

    for m, (qm, (lo, hi)) in enumerate(zip(qs, halves)):
        s = _scores(qm, km_ref[:, lo:hi]) + bm_ref[0, jnp.minimum(qi, 1)]
        m0 = row_max(s)
        p = jnp.exp(s - m0[:, 0:N_META])
        m_scr[m] = m0
        acc_scr[m] = jnp.dot(p.astype(BF16), with_ones(vm), preferred_element_type=F32)

    def qk(j, near):
        start = pl.multiple_of(j * tq, tq)
        ss = []
        for qm, (lo, hi) in zip(qs, halves):
            s = _scores(qm, k_ref[0, pl.ds(start, tq), lo:hi])
            if near:
                s = s + bias_ref[0, j - qi + 1]
            ss.append(s)
        return ss

    def update(j, ss, near):
        start = pl.multiple_of(j * tq, tq)
        vb = with_ones(v_ref[0, pl.ds(start, tq), :])
        shift = 0.0 if near else far
        mbs = []
        for m in range(2):
            m_old = m_scr[m]
            m_new = jnp.maximum(m_old, row_max(ss[m]) + shift)
            m_scr[m] = m_new
            acc_scr[m] = lane_tiles(jnp.exp(m_old - m_new), 2) * acc_scr[m]
            mbs.append(lane_tiles(m_new - shift, tq // LANES))
        for m in range(2):
            ps = []
            for c in range(tq // ATTN_RC):
                rs = slice(c * ATTN_RC, (c + 1) * ATTN_RC)
                ps.append(jnp.exp(ss[m][rs, :] - mbs[m][rs, :]).astype(BF16))
            acc_scr[m] += jnp.dot(jnp.concatenate(ps, axis=0), vb, preferred_element_type=F32)

    def group(size, first, near):
        def body(i, _):
            j0 = first + size * i
            ss = qk(j0, near)
            for g in range(size):
                nxt = qk(j0 + g + 1, near) if g + 1 < size else None
                update(j0 + g, ss, near)
                ss = nxt
            return 0
        return body

    done = 0
    for size in ATTN_GROUPS:
        trips = (n_far - done) // size
        lax.fori_loop(0, trips, group(size, done, False), 0)
        done = done + trips * size
    has_prev = jnp.minimum(qi, 1)
    lax.fori_loop(0, has_prev, group(2, qi - 1, True), 0)
    lax.fori_loop(0, 1 - has_prev, group(1, qi, True), 0)
    outs = [acc_scr[m, :, 0:V_DIM] / acc_scr[m, :, V_DIM:2 * V_DIM] for m in range(2)]
    o_ref[0] = _diff_out(outs[0], outs[1], lam, g_ref[...], out_scale)


def _prompt_attention(scal, q, kb, vb, km, vm, bias3, bm2, subln_g, tq, out_scale):
    bsz, t, d = q.shape
    return pl.pallas_call(
        functools.partial(_prompt_attn_kernel, tq=tq, out_scale=out_scale),
        grid=(bsz, N_HEADS, t // tq),
        in_specs=[pl.BlockSpec(memory_space=pltpu.SMEM),
                  pl.BlockSpec((1, tq, V_DIM), lambda b, h, i: (b, i, h)),
                  pl.BlockSpec((1, t, V_DIM), lambda b, h, i: (b, 0, h)),
                  pl.BlockSpec((1, t, V_DIM), lambda b, h, i: (b, 0, h)),
                  pl.BlockSpec((N_META, V_DIM), lambda b, h, i: (0, h)),
                  pl.BlockSpec((N_META, V_DIM), lambda b, h, i: (0, h)),
                  pl.BlockSpec((1, 2, tq, tq), lambda b, h, i: (h, 0, 0, 0)),
                  pl.BlockSpec((1, 2, tq, N_META), lambda b, h, i: (h, 0, 0, 0)),
                  _const_spec((1, V_DIM))],
        out_specs=pl.BlockSpec((1, tq, V_DIM), lambda b, h, i: (b, i, h)),
        out_shape=jax.ShapeDtypeStruct((bsz, t, d), F32),
        scratch_shapes=[pltpu.VMEM((2, tq, LANES), F32), pltpu.VMEM((2, tq, V_DIM + LANES), F32)],
        compiler_params=_params("parallel", "parallel", "arbitrary"),
        name="prompt_attn",
    )(scal, q, kb, vb, km, vm, bias3, bm2, subln_g)


def _sample_attn_kernel(sc_ref, q_ref, kc_ref, vc_ref, kn_ref, vn_ref, bc_ref, bn_ref, g_ref, o_ref,
                        *, out_scale):
    lam = sc_ref[0]
    q = q_ref[0]
    kc = kc_ref[0].astype(BF16)
    vc = vc_ref[0].astype(BF16)
    vn = vn_ref[0]
    outs = []
    for m in range(2):
        lo, hi = m * HEAD_DIM, (m + 1) * HEAD_DIM
        qm = q[:, lo:hi]
        s_c = _scores(qm, kc[:, lo:hi]) + bc_ref[0]
        s_n = _scores(qm, kn_ref[0, :, lo:hi]) + bn_ref[0]
        mx = jnp.maximum(jnp.max(s_c, axis=-1, keepdims=True), jnp.max(s_n, axis=-1, keepdims=True))
        p_c = jnp.exp(s_c - mx)
        p_n = jnp.exp(s_n - mx)
        l = jnp.sum(p_c, axis=-1, keepdims=True) + jnp.sum(p_n, axis=-1, keepdims=True)
        acc = (jnp.dot(p_c.astype(BF16), vc, preferred_element_type=F32)
               + jnp.dot(p_n.astype(BF16), vn, preferred_element_type=F32))
        outs.append(acc / l)
    o_ref[0] = _diff_out(outs[0], outs[1], lam, g_ref[...], out_scale)


def _sample_attention(scal, q, kc, vc, kn, vn, bias_c, bias_n, subln_g, out_scale):
    bsz, ts, d = q.shape
    past = kc.shape[1]
    per_bh = lambda n: pl.BlockSpec((1, n, V_DIM), lambda b, h: (b, 0, h))
    return pl.pallas_call(
        functools.partial(_sample_attn_kernel, out_scale=out_scale),
        grid=(bsz, N_HEADS),
        in_specs=[pl.BlockSpec(memory_space=pltpu.SMEM),
                  per_bh(ts), per_bh(past), per_bh(past), per_bh(ts), per_bh(ts),
                  pl.BlockSpec((1, ts, past), lambda b, h: (h, 0, 0)),
                  pl.BlockSpec((1, ts, ts), lambda b, h: (h, 0, 0)),
                  _const_spec((1, V_DIM))],
        out_specs=per_bh(ts),
        out_shape=jax.ShapeDtypeStruct((bsz, ts, d), F32),
        compiler_params=_params("parallel", "parallel"),
        name="sample_attn",
    )(scal, q, kc, vc, kn, vn, bias_c, bias_n, subln_g)


def _merge_kernel(h_ref, c_ref, a_ref, gs_ref, w_ref, g_ref, b_ref, o_ref, ob_ref, *, d):
    mix = gs_ref[:, 0:d] * c_ref[...] + gs_ref[:, d:2 * d] * a_ref[...]
    y = jnp.dot(mix.astype(BF16), w_ref[...], preferred_element_type=F32)
    h1 = _layer_norm(ALPHA * h_ref[...] + y, g_ref[...], b_ref[...])
    o_ref[...] = h1
    ob_ref[...] = h1.astype(BF16)


def _merge(h, conv_out, attn_out, gs, w_b, ln_g, ln_b, tm):
    r, d = h.shape
    row = lambda n: pl.BlockSpec((tm, n), lambda i: (i, 0))
    return pl.pallas_call(
        functools.partial(_merge_kernel, d=d),
        grid=(r // tm,),
        in_specs=[row(d), row(d), row(d), row(2 * d), _const_spec((d, d)), _const_spec((1, d)),
                  _const_spec((1, d))],
        out_specs=(row(d), row(d)),
        out_shape=(jax.ShapeDtypeStruct((r, d), F32), jax.ShapeDtypeStruct((r, d), BF16)),
        compiler_params=_params("parallel"),
        name="merge_ln1",
    )(h, conv_out, attn_out, gs, w_b, ln_g, ln_b)


def _top16_sorted(s, rowid16):
    work = s
    top = jnp.full(rowid16.shape, -jnp.inf, F32)
    for r in range(PEER_TOPK):
        mx = jnp.max(work, axis=0, keepdims=True)
        top = jnp.where(rowid16 == r, mx, top)
        if r + 1 < PEER_TOPK:
            work = jnp.where(work == mx, -jnp.inf, work)
    return top


def _peer_prep_kernel(x_ref, wq_ref, sk_ref, p_ref, t_ref, q_scr, *, tb):
    q_scr[...] = _scores(wq_ref[...], x_ref[...]).astype(BF16)
    rowid16 = lax.broadcasted_iota(jnp.int32, (PEER_TOPK, tb), 0)
    rowid8 = lax.broadcasted_iota(jnp.int32, (SUBLANES, tb), 0)

    def head(h, _):
        base = pl.multiple_of(h * 2 * PEER_HALF, 2 * PEER_HALF)
        s0 = jnp.dot(sk_ref[0], q_scr[pl.ds(base, PEER_HALF), :], preferred_element_type=F32)
        s1 = jnp.dot(sk_ref[1], q_scr[pl.ds(base + PEER_HALF, PEER_HALF), :], preferred_element_type=F32)
        a = _top16_sorted(s0, rowid16)
        b = _top16_sorted(s1, rowid16)
        cands = [a[0:1] + b[0:8], a[0:1] + b[8:16]]
        for i in range(1, 8):
            cands.append(jnp.where(rowid8 < PEER_TOPK // (i + 1), a[i:i + 1] + b[0:8], -jnp.inf))
        cands.append(a[8:16] + b[0:1])
        top = _top16_sorted(jnp.concatenate(cands, axis=0), rowid16)
        z = jnp.sum(jnp.exp(top - top[0:1]), axis=0, keepdims=True)
        p_ref[h, 0] = s0
        p_ref[h, 1] = jnp.exp(s0 - a[0:1]) / z
        p_ref[h, 2] = s1
        p_ref[h, 3] = jnp.exp(s1 - b[0:1])
        t_ref[h] = jnp.broadcast_to(top[PEER_TOPK - 1:PEER_TOPK], (SUBLANES, tb))
        return 0

    lax.fori_loop(0, PEER_HEADS, head, 0, unroll=4)


def _peer_prep(x, wq_t, sub_keys_b, tb):
    r, d = x.shape
    nq = wq_t.shape[0]
    return pl.pallas_call(
        functools.partial(_peer_prep_kernel, tb=tb),
        grid=(r // tb,),
        in_specs=[pl.BlockSpec((tb, d), lambda i: (i, 0)), _const_spec((nq, d)),
                  _const_spec((2, N_KEYS, PEER_HALF))],
        out_specs=(pl.BlockSpec((PEER_HEADS, 4, N_KEYS, tb), lambda i: (0, 0, 0, i)),
                   pl.BlockSpec((PEER_HEADS, SUBLANES, tb), lambda i: (0, 0, i))),
        out_shape=(jax.ShapeDtypeStruct((PEER_HEADS, 4, N_KEYS, r), F32),
                   jax.ShapeDtypeStruct((PEER_HEADS, SUBLANES, r), F32)),
        scratch_shapes=[pltpu.VMEM((nq, tb), BF16)],
        compiler_params=_params("parallel"),
        name="peer_prep",
    )(x, wq_t, sub_keys_b)


PEER_JT = 32
PEER_SUB = 256
PEER_ZR = 64


def _peer_kernel(unit_ref, x_ref, xb_ref, p_ref, t_ref, u_ref, vt_ref, g_ref, b_ref, o_ref,
                 acc_ref, pre_ref, w_ref, z_ref, vprev_ref, *, ec, tb):
    e = pl.program_id(1)

    @pl.when(e == 0)
    def _():
        acc_ref[...] = jnp.zeros_like(acc_ref)
        z_ref[...] = jnp.zeros_like(z_ref)
        vprev_ref[...] = jnp.zeros_like(vprev_ref)

    n_sub = ec // PEER_SUB
    d = u_ref.shape[1]
    half = tb // 2
    n_k = d // MXU_TILE

    def pre_piece(sc, p):
        n, k = divmod(p, n_k)
        ks = slice(k * MXU_TILE, (k + 1) * MXU_TILE)
        return n, _scores(u_ref[sc * PEER_SUB:(sc + 1) * PEER_SUB, ks], xb_ref[n * half:(n + 1) * half, ks])

    def out_piece(sp, n, v_ref):
        ns = slice(n * half, (n + 1) * half)
        es = slice(sp * 2 * PEER_SUB, (sp + 1) * 2 * PEER_SUB)
        acc_ref[:, ns] += jnp.dot(v_ref[:, es], z_ref[es, ns], preferred_element_type=F32)

    def add_piece(pre, n, piece):
        pre[n] = piece if pre[n] is None else pre[n] + piece

    n_grp = ec // N_KEYS
    i0 = e * n_grp
    s0_rows = [[p_ref[h, 0, pl.ds(i0 + gi, 1), :] for h in range(PEER_HEADS)] for gi in range(n_grp)]
    ea_rows = [[p_ref[h, 1, pl.ds(i0 + gi, 1), :] for h in range(PEER_HEADS)] for gi in range(n_grp)]
    tiles = [(lc, jt) for lc in range(tb // LANES) for jt in range(N_KEYS // PEER_JT)]
    pre_jobs = [(sc, p) for sc in range(n_sub) for p in range(2 * n_k)]
    out_jobs = [(sp, n) for sp in range(ec // (2 * PEER_SUB)) for n in range(2)]
    assert len(pre_jobs) % len(tiles) == 0 and len(tiles) % len(out_jobs) == 0
    pre_per_tile = len(pre_jobs) // len(tiles)
    tiles_per_out = len(tiles) // len(out_jobs)
    pre_acc = [None, None]
    for it, (lc, jt) in enumerate(tiles):
        jobs = [("pre", j) for j in pre_jobs[it * pre_per_tile:(it + 1) * pre_per_tile]]
        if it % tiles_per_out == 0:
            jobs.insert(len(jobs) // 2, ("out", out_jobs[it // tiles_per_out]))
        before_head = {}
        for q, job in enumerate(jobs):
            before_head.setdefault(q * PEER_HEADS // len(jobs) + 1, []).append(job)
        lanes = slice(lc * LANES, (lc + 1) * LANES)
        js = slice(jt * PEER_JT, (jt + 1) * PEER_JT)
        w = [jnp.zeros((PEER_JT, LANES), F32) for _ in range(n_grp)]
        for h in range(PEER_HEADS):
            for kind, job in before_head.get(h, []):
                if kind == "out":
                    out_piece(job[0], job[1], vprev_ref)
                else:
                    n, piece = pre_piece(*job)
                    add_piece(pre_acc, n, piece)
                    if job[1] % n_k == n_k - 1:
                        pre_ref[job[0] * PEER_SUB:(job[0] + 1) * PEER_SUB, n * half:(n + 1) * half] = pre_acc[n]
                        pre_acc[n] = None
            s1 = p_ref[h, 2, js, lanes] * unit_ref[0]
            eb = p_ref[h, 3, js, lanes] * unit_ref[0]
            thr = t_ref[h, 0:1, lanes]
            for gi in range(n_grp):
                sel = jnp.where(s0_rows[gi][h][:, lanes] + s1 >= thr, eb, 0.0)
                w[gi] = w[gi] + sel * ea_rows[gi][h][:, lanes]
        for gi in range(n_grp):
            w_ref[gi * N_KEYS + jt * PEER_JT:gi * N_KEYS + (jt + 1) * PEER_JT, lanes] = w[gi]

    for r0 in range(0, ec, PEER_ZR):
        pre = pre_ref[r0:r0 + PEER_ZR, :]
        act = 0.5 * pre * (1.0 + lax.erf(pre * (2.0 ** -0.5)))
        z_ref[r0:r0 + PEER_ZR, :] = (w_ref[r0:r0 + PEER_ZR, :] * act).astype(BF16)
    vprev_ref[...] = vt_ref[...]

    @pl.when(e == pl.num_programs(1) - 1)
    def _():
        for sp, n in out_jobs:
            out_piece(sp, n, vt_ref)
        y = acc_ref[...].T
        o_ref[...] = _layer_norm(ALPHA * x_ref[...] + y, g_ref[...], b_ref[...])


def _peer(x, xb, pfac, thr, u_b, vt_b, ln_g, ln_b, tb, ec):
    r, d = x.shape
    n_exp = u_b.shape[0]
    return pl.pallas_call(
        functools.partial(_peer_kernel, ec=ec, tb=tb),
        grid=(r // tb, n_exp // ec),
        in_specs=[pl.BlockSpec(memory_space=pltpu.SMEM),
                  pl.BlockSpec((tb, d), lambda i, e: (i, 0)),
                  pl.BlockSpec((tb, d), lambda i, e: (i, 0)),
                  pl.BlockSpec((PEER_HEADS, 4, N_KEYS, tb), lambda i, e: (0, 0, 0, i)),
                  pl.BlockSpec((PEER_HEADS, SUBLANES, tb), lambda i, e: (0, 0, i)),
                  pl.BlockSpec((ec, d), lambda i, e: (e, 0)),
                  pl.BlockSpec((d, ec), lambda i, e: (0, e)),
                  _const_spec((1, d)), _const_spec((1, d))],
        out_specs=pl.BlockSpec((tb, d), lambda i, e: (i, 0)),
        out_shape=jax.ShapeDtypeStruct((r, d), F32),
        scratch_shapes=[pltpu.VMEM((d, tb), F32), pltpu.VMEM((ec, tb), F32), pltpu.VMEM((ec, tb), F32),
                        pltpu.VMEM((ec, tb), BF16), pltpu.VMEM((d, ec), BF16)],
        compiler_params=_params("parallel", "arbitrary"),
        name="peer_experts",
    )(jnp.ones((1,), F32), x, xb, pfac, thr, u_b, vt_b, ln_g, ln_b)


def _t5_bucket(rel):
    half = NUM_BUCKETS // 2
    max_exact = half // 2
    n = jnp.abs(rel)
    large = max_exact + (jnp.log(jnp.maximum(n, 1).astype(F32) / max_exact)
                         / math.log(MAX_DISTANCE / max_exact) * (half - max_exact)).astype(jnp.int32)
    large = jnp.minimum(large, half - 1)
    return jnp.where(rel > 0, half, 0) + jnp.where(n < max_exact, n, large)


def _rel_bias_tile(rel_bias, rel):
    bucket = _t5_bucket(rel)[None]
    tile = jnp.zeros((rel_bias.shape[1],) + rel.shape, F32)
    for b in range(NUM_BUCKETS):
        tile = jnp.where(bucket == b, rel_bias[b].astype(F32)[:, None, None], tile)
    return tile


def _pick(n, prefs):
    for p in prefs:
        if n % p == 0:
            return p
    raise ValueError(f"no supported tile for extent {n}")


def kernel(x_prompt, x_sample, cache_k, cache_v, state_conv, meta_tokens, ln_in_g, ln_in_b, rel_bias, w_in, b_gate, conv_w, conv_b, conv_ln_g, conv_ln_b, w_conv_out, lam_q1, lam_k1, lam_q2, lam_k2, subln_g, w_out, ln1_g, ln1_b, w_peer_q, sub_keys, peer_u, peer_v, ln2_g, ln2_b):
    bsz, t, d = x_prompt.shape
    bd, ts, _ = x_sample.shape
    past = cache_k.shape[2]
    assert w_in.shape[0] == DEPTH == 1
    assert t >= CONV_WIDTH - 1 and ts >= CONV_WIDTH - 1 and ts % CTX_ROWS == 0
    l = 0
    row2 = lambda a: a.reshape(1, -1)

    lam_init = 0.8 - 0.6 * math.exp(-0.3 * l)
    lam = (jnp.exp(jnp.sum(lam_q1[l] * lam_k1[l]).astype(F32))
           - jnp.exp(jnp.sum(lam_q2[l] * lam_k2[l]).astype(F32)) + lam_init)
    scal = jnp.reshape(lam, (1,)).astype(F32)
    out_scale = 1.0 - lam_init

    w_in_b = w_in[l].astype(BF16)
    ln_g, ln_b = row2(ln_in_g), row2(ln_in_b)
    bg = row2(b_gate[l])

    tm = _pick(bsz * t, (256, 128, 64, 32, 16, 8))
    n_s = bd * ts
    tm_s = 256
    pad = (-(n_s + N_META)) % tm_s
    x_small = jnp.concatenate([x_sample.reshape(n_s, d), meta_tokens.astype(F32), jnp.zeros((pad, d), F32)], axis=0)
    hp, up, qp, kp, vp, kpb, vpb, gsp = _inproj(x_prompt.reshape(bsz * t, d), ln_g, ln_b, w_in_b, bg, tm)
    hs, us, qs, ks, vs, ksb, vsb, gss = _inproj(x_small, ln_g, ln_b, w_in_b, bg, tm_s)
    meta = slice(n_s, n_s + N_META)
    as3 = lambda a, b_: a.reshape(b_, -1, a.shape[-1])

    cw = jnp.concatenate([conv_w[l], jnp.zeros((CTX_ROWS - CONV_WIDTH, d), F32)], axis=0)
    conv_args = (cw, row2(conv_b[l]), row2(conv_ln_g[l]), row2(conv_ln_b[l]), w_conv_out[l].astype(BF16))
    ctx_p = jnp.concatenate([jnp.zeros((CTX_ROWS - N_META, d), F32), us[meta]], axis=0)[None]
    ctx_s = jnp.concatenate([jnp.zeros((bd, CTX_PAD, d), F32), state_conv[l]], axis=1)
    up3 = as3(up, bsz)
    us3 = as3(us[:n_s], bd)
    conv_p = _conv_branch(up3, ctx_p, *conv_args, tt=_pick(t, (256, 128, 64, 32)))
    conv_s = _conv_branch(us3, ctx_s, *conv_args, tt=ts)

    tq = _pick(t, (512, 256, 128, 64))
    assert tq % CHUNK == 0 and tq + 1 >= MAX_DISTANCE
    fq = jnp.arange(tq)
    rel_own = fq[None, :] - fq[:, None]
    own = _rel_bias_tile(rel_bias, rel_own) + jnp.where(fq[None, :] // CHUNK <= fq[:, None] // CHUNK, 0.0, NEG_INF)
    prev = _rel_bias_tile(rel_bias, rel_own - tq)
    far = _rel_bias_tile(rel_bias, jnp.full((1, 1), -(tq + 1)))
    bias3 = jnp.stack([prev, own], axis=1)
    scal = jnp.concatenate([scal, far.reshape(N_HEADS)])
    rel_meta = jnp.arange(N_META)[None, :] - (N_META + fq[:, None])
    bm = _rel_bias_tile(rel_bias, rel_meta)
    bm2 = jnp.stack([bm, jnp.broadcast_to(far, bm.shape)], axis=1)
    sg = row2(subln_g[l])
    attn_p = _prompt_attention(scal, as3(qp, bsz), as3(kpb, bsz), as3(vpb, bsz), ksb[meta], vsb[meta],
                               bias3, bm2, sg, tq, out_scale)

    k_pos = jnp.arange(past + ts)
    rel_s = k_pos[None, :] - k_pos[past:, None]
    bias_s = _rel_bias_tile(rel_bias, rel_s)
    attn_s = _sample_attention(scal, as3(qs[:n_s], bd), cache_k[l].reshape(bd, past, d),
                               cache_v[l].reshape(bd, past, d), as3(ksb[:n_s], bd), as3(vsb[:n_s], bd),
                               bias_s[:, :, :past], bias_s[:, :, past:], sg, out_scale)

    w_out_b = w_out[l].astype(BF16)
    h1p = _merge(hp, conv_p.reshape(-1, d), attn_p.reshape(-1, d), gsp, w_out_b, row2(ln1_g[l]), row2(ln1_b[l]), tm)
    h1s = _merge(hs[:n_s], conv_s.reshape(-1, d), attn_s.reshape(-1, d), gss[:n_s], w_out_b,
                 row2(ln1_g[l]), row2(ln1_b[l]), _pick(n_s, (256, 128, 64, 32, 16, 8)))

    wq_t = w_peer_q[l].T.astype(BF16)
    sk_b = sub_keys[l].astype(BF16)
    u_b = peer_u[l].astype(BF16)
    vt_b = peer_v[l].T.astype(BF16)

    def peer_ln2(h1, h1b):
        tb = _pick(h1.shape[0], (512, 256, 128))
        pfac, thr = _peer_prep(h1b, wq_t, sk_b, _pick(h1.shape[0], (256, 128)))
        return _peer(h1, h1b, pfac, thr, u_b, vt_b, row2(ln2_g[l]), row2(ln2_b[l]), tb, ec=1024)

    y_p = peer_ln2(*h1p).reshape(bsz, t, d)
    y_s = peer_ln2(*h1s).reshape(bd, ts, d)

    heads = lambda a, b_: a.reshape(1, b_, -1, N_HEADS, V_DIM)
    with_meta = lambda rows, m: jnp.concatenate([jnp.broadcast_to(m[None], (bsz, N_META, d)), as3(rows, bsz)], axis=1)
    keep = CONV_WIDTH - 1
    return (y_p, y_s,
            heads(with_meta(kp, ks[meta]), bsz), heads(with_meta(vp, vs[meta]), bsz), up3[None, :, t - keep:],
            heads(ks[:n_s], bd), heads(vs[:n_s], bd), us3[None, :, ts - keep:])
```

```python
import functools
import math

import jax
import jax.numpy as jnp
from jax import lax
from jax.experimental import pallas as pl
from jax.experimental.pallas import tpu as pltpu

F32 = jnp.float32
BF16 = jnp.bfloat16

DEPTH = 1
CHUNK = 64
N_META = 16
CONV_WIDTH = 31
N_HEADS = 8
HEAD_DIM = 64
V_DIM = 2 * HEAD_DIM
NUM_BUCKETS = 32
MAX_DISTANCE = 128
PEER_HEADS = 8
PEER_TOPK = 16
N_KEYS = 128
PEER_HALF = 128
ALPHA = (2.0 * DEPTH) ** 0.25
LN_EPS = 1e-5
NEG_INF = -1e30

SUBLANES = 8
LANES = 128
MXU_TILE = 256
VMEM_LIMIT = 56 * 1024 * 1024

CTX_ROWS = 32
CTX_PAD = CTX_ROWS - (CONV_WIDTH - 1)


def _layer_norm(x, g, b):
    mu = jnp.mean(x, axis=-1, keepdims=True)
    xc = x - mu
    var = jnp.mean(xc * xc, axis=-1, keepdims=True)
    return xc * lax.rsqrt(var + LN_EPS) * g + b


def _params(*sem):
    return pltpu.CompilerParams(dimension_semantics=sem, vmem_limit_bytes=VMEM_LIMIT)


def _const_spec(shape):
    nd = len(shape)
    return pl.BlockSpec(shape, lambda *_: (0,) * nd)


def _inproj_kernel(x_ref, g_ref, b_ref, w_ref, bg_ref,
                   h_ref, u_ref, q_ref, k_ref, v_ref, kb_ref, vb_ref, gs_ref, *, d):
    h = _layer_norm(x_ref[...], g_ref[...], b_ref[...])
    h_ref[...] = h
    hb = h.astype(BF16)

    def proj(c0, n):
        return jnp.dot(hb, w_ref[:, c0:c0 + n], preferred_element_type=F32)

    u_ref[...] = proj(0, d) * jax.nn.sigmoid(proj(d, d))
    q_ref[...] = (proj(2 * d, d) * (HEAD_DIM ** -0.5)).astype(BF16)
    k = proj(3 * d, d)
    k_ref[...] = k
    kb_ref[...] = k.astype(BF16)
    v = proj(4 * d, d)
    v_ref[...] = v
    vb_ref[...] = v.astype(BF16)
    gs_ref[...] = jax.nn.sigmoid(proj(5 * d, 2 * d) + bg_ref[...])


def _inproj(x, ln_g, ln_b, w_in_b, b_gate, tm):
    r, d = x.shape
    d_in = w_in_b.shape[1]
    row = lambda n: pl.BlockSpec((tm, n), lambda i: (i, 0))
    out_shape = (
        jax.ShapeDtypeStruct((r, d), F32),
        jax.ShapeDtypeStruct((r, d), F32),
        jax.ShapeDtypeStruct((r, d), BF16),
        jax.ShapeDtypeStruct((r, d), F32),
        jax.ShapeDtypeStruct((r, d), F32),
        jax.ShapeDtypeStruct((r, d), BF16),
        jax.ShapeDtypeStruct((r, d), BF16),
        jax.ShapeDtypeStruct((r, 2 * d), F32),
    )
    return pl.pallas_call(
        functools.partial(_inproj_kernel, d=d),
        grid=(r // tm,),
        in_specs=[row(d), _const_spec((1, d)), _const_spec((1, d)),
                  pl.BlockSpec((d, d_in), lambda i: (0, 0), pipeline_mode=pl.Buffered(1)),
                  _const_spec((1, 2 * d))],
        out_specs=(row(d), row(d), row(d), row(d), row(d), row(d), row(d), row(2 * d)),
        out_shape=out_shape,
        compiler_params=_params("parallel"),
        name="in_proj",
    )(x, ln_g, ln_b, w_in_b, b_gate)


CONV_RC = 32
CONV_CC = 256


def _conv_kernel(u_ref, ctx_ref, cw_ref, cb_ref, g_ref, b_ref, w_ref, o_ref, buf_ref, sh_ref, c_ref, *, tt, d):
    @pl.when(pl.program_id(1) == 0)
    def _():
        buf_ref[0:CTX_ROWS, :] = ctx_ref[0]

    buf_ref[CTX_ROWS:CTX_ROWS + tt, :] = u_ref[0]
    n_sh = tt + CTX_ROWS - SUBLANES
    for r in range(1, SUBLANES):
        sh_ref[r - 1, 0:n_sh, :] = buf_ref[r:r + n_sh, :]
    for r0 in range(0, tt, CONV_RC):
        for c0 in range(0, d, CONV_CC):
            acc = jnp.broadcast_to(cb_ref[:, c0:c0 + CONV_CC], (CONV_RC, CONV_CC))
            for w in range(CONV_WIDTH):
                a, r = divmod(CTX_PAD + w, SUBLANES)
                s = r0 + a * SUBLANES
                rows = (buf_ref[s:s + CONV_RC, c0:c0 + CONV_CC] if r == 0
                        else sh_ref[r - 1, s:s + CONV_RC, c0:c0 + CONV_CC])
                acc = acc + rows * cw_ref[w:w + 1, c0:c0 + CONV_CC]
            c_ref[r0:r0 + CONV_RC, c0:c0 + CONV_CC] = acc
    buf_ref[0:CTX_ROWS, :] = buf_ref[tt:tt + CTX_ROWS, :]
    y = _layer_norm(c_ref[...], g_ref[...], b_ref[...])
    y = y * jax.nn.sigmoid(y)
    o_ref[0] = jnp.dot(y.astype(BF16), w_ref[...], preferred_element_type=F32)


def _conv_branch(u, ctx, cw, cb, ln_g, ln_b, w_b, tt):
    bsz, t, d = u.shape
    nctx = ctx.shape[0]
    ctx_map = (lambda b, i: (b, 0, 0)) if nctx > 1 else (lambda b, i: (0, 0, 0))
    return pl.pallas_call(
        functools.partial(_conv_kernel, tt=tt, d=d),
        grid=(bsz, t // tt),
        in_specs=[pl.BlockSpec((1, tt, d), lambda b, i: (b, i, 0)),
                  pl.BlockSpec((1, CTX_ROWS, d), ctx_map),
                  _const_spec((CTX_ROWS, d)), _const_spec((1, d)), _const_spec((1, d)), _const_spec((1, d)),
                  _const_spec((d, d))],
        out_specs=pl.BlockSpec((1, tt, d), lambda b, i: (b, i, 0)),
        out_shape=jax.ShapeDtypeStruct((bsz, t, d), F32),
        scratch_shapes=[pltpu.VMEM((CTX_ROWS + tt, d), F32),
                        pltpu.VMEM((SUBLANES - 1, CTX_ROWS + tt - SUBLANES, d), F32),
                        pltpu.VMEM((tt, d), F32)],
        compiler_params=_params("parallel", "arbitrary"),
        name="conv_branch",
    )(u, ctx, cw, cb, ln_g, ln_b, w_b)


ATTN_RC = 32
ATTN_GROUPS = (4, 2, 1)


def _softmax_step(carry, s, v):
    m_old, l_old, acc_old = carry
    m_new = jnp.maximum(m_old, jnp.max(s, axis=-1, keepdims=True))
    alpha = jnp.exp(m_old - m_new)
    p = jnp.exp(s - m_new)
    l_new = alpha * l_old + jnp.sum(p, axis=-1, keepdims=True)
    acc_new = alpha * acc_old + jnp.dot(p.astype(BF16), v, preferred_element_type=F32)
    return m_new, l_new, acc_new


def _scores(q, k):
    return lax.dot_general(q, k, (((1,), (1,)), ((), ())), preferred_element_type=F32)


def _diff_out(o1, o2, lam, g, out_scale):
    o = o1 - lam * o2
    o = o * lax.rsqrt(jnp.mean(o * o, axis=-1, keepdims=True) + LN_EPS)
    return o * g * out_scale


def _prompt_attn_kernel(sc_ref, q_ref, k_ref, v_ref, km_ref, vm_ref, bias_ref, bm_ref, g_ref, o_ref,
                        m_scr, acc_scr, *, tq, out_scale):
    qi = pl.program_id(2)
    lam = sc_ref[0]
    far = sc_ref[1 + pl.program_id(1)]
    vm = vm_ref[...]
    halves = [(m * HEAD_DIM, (m + 1) * HEAD_DIM) for m in range(2)]
    n_far = jnp.maximum(qi - 1, 0)
    q = q_ref[0]
    qs = [q[:, lo:hi] for lo, hi in halves]

    def with_ones(v):
        return jnp.concatenate([v, jnp.ones((v.shape[0], LANES), BF16)], axis=1)

    def lane_tiles(x, n):
        return jnp.concatenate([x] * n, axis=1)

    def row_max(s):
        return jnp.broadcast_to(jnp.max(s, axis=-1, keepdims=True), (s.shape[0], LANES))

    m_scr[...] = jnp.full(m_scr.shape, NEG_INF, F32)
    acc_scr[...] = jnp.zeros(acc_scr.shape, F32)

    def qk(j, near):
        start = pl.multiple_of(j * tq, tq)
        ss = []
        for qm, (lo, hi) in zip(qs, halves):
            s = _scores(qm, k_ref[0, pl.ds(start, tq), lo:hi])
            if near:
                s = s + bias_ref[0, j - qi + 1]
            ss.append(s)
        return ss

    def qk_meta():
        return [_scores(qm, km_ref[:, lo:hi]) + bm_ref[0, jnp.minimum(qi, 1)] for qm, (lo, hi) in zip(qs, halves)]

    def update(j, ss, near, s_meta=None):
        start = pl.multiple_of(j * tq, tq)
        vb = with_ones(v_ref[0, pl.ds(start, tq), :])
        shift = 0.0 if near else far
        mbs = []
        for m in range(2):
            m_old = m_scr[m]
            m_new = jnp.maximum(m_old, row_max(ss[m]) + shift)
            if s_meta is not None:
                m_new = jnp.maximum(m_new, row_max(s_meta[m]))
            m_scr[m] = m_new
            acc_scr[m] = lane_tiles(jnp.exp(m_old - m_new), 2) * acc_scr[m]
            mbs.append(lane_tiles(m_new - shift, tq // LANES))
        for m in range(2):
            ps = []
            for c in range(tq // ATTN_RC):
                rs = slice(c * ATTN_RC, (c + 1) * ATTN_RC)
                ps.append(jnp.exp(ss[m][rs, :] - mbs[m][rs, :]).astype(BF16))
            part = jnp.dot(jnp.concatenate(ps, axis=0), vb, preferred_element_type=F32)
            if s_meta is not None:
                p_meta = jnp.exp(s_meta[m] - mbs[m][:, 0:N_META]).astype(BF16)
                part = part + jnp.dot(p_meta, with_ones(vm), preferred_element_type=F32)
            acc_scr[m] += part

    def group(size, first, near):
        def body(i, _):
            j0 = first + size * i
            ss = qk(j0, near)
            s_meta = qk_meta() if near else None
            for g in range(size):
                nxt = qk(j0 + g + 1, near) if g + 1 < size else None
                update(j0 + g, ss, near, s_meta if g + 1 == size else None)
                ss = nxt
            return 0
        return body

    done = 0
    for size in ATTN_GROUPS:
        trips = (n_far - done) // size
        lax.fori_loop(0, trips, group(size, done, False), 0)
        done = done + trips * size
    has_prev = jnp.minimum(qi, 1)
    lax.fori_loop(0, has_prev, group(2, qi - 1, True), 0)
    lax.fori_loop(0, 1 - has_prev, group(1, qi, True), 0)
    outs = [acc_scr[m, :, 0:V_DIM] / acc_scr[m, :, V_DIM:2 * V_DIM] for m in range(2)]
    o_ref[0] = _diff_out(outs[0], outs[1], lam, g_ref[...], out_scale)


def _prompt_attention(scal, q, kb, vb, km, vm, bias3, bm2, subln_g, tq, out_scale):
    bsz, t, d = q.shape
    return pl.pallas_call(
        functools.partial(_prompt_attn_kernel, tq=tq, out_scale=out_scale),
        grid=(bsz, N_HEADS, t // tq),
        in_specs=[pl.BlockSpec(memory_space=pltpu.SMEM),
                  pl.BlockSpec((1, tq, V_DIM), lambda b, h, i: (b, i, h)),
                  pl.BlockSpec((1, t, V_DIM), lambda b, h, i: (b, 0, h)),
                  pl.BlockSpec((1, t, V_DIM), lambda b, h, i: (b, 0, h)),
                  pl.BlockSpec((N_META, V_DIM), lambda b, h, i: (0, h)),
                  pl.BlockSpec((N_META, V_DIM), lambda b, h, i: (0, h)),
                  pl.BlockSpec((1, 2, tq, tq), lambda b, h, i: (h, 0, 0, 0)),
                  pl.BlockSpec((1, 2, tq, N_META), lambda b, h, i: (h, 0, 0, 0)),
                  _const_spec((1, V_DIM))],
        out_specs=pl.BlockSpec((1, tq, V_DIM), lambda b, h, i: (b, i, h)),
        out_shape=jax.ShapeDtypeStruct((bsz, t, d), F32),
        scratch_shapes=[pltpu.VMEM((2, tq, LANES), F32), pltpu.VMEM((2, tq, V_DIM + LANES), F32)],
        compiler_params=_params("parallel", "parallel", "arbitrary"),
        name="prompt_attn",
    )(scal, q, kb, vb, km, vm, bias3, bm2, subln_g)


def _sample_attn_kernel(sc_ref, q_ref, kc_ref, vc_ref, kn_ref, vn_ref, bc_ref, bn_ref, g_ref, o_ref,
                        *, out_scale):
    lam = sc_ref[0]
    for h in range(N_HEADS):
        cols = slice(h * V_DIM, (h + 1) * V_DIM)
        q = q_ref[0, :, cols]
        kc = kc_ref[0, :, h, :].astype(BF16)
        vc = vc_ref[0, :, h, :].astype(BF16)
        vn = vn_ref[0, :, cols]
        outs = []
        for m in range(2):
            lo, hi = m * HEAD_DIM, (m + 1) * HEAD_DIM
            qm = q[:, lo:hi]
            s_c = _scores(qm, kc[:, lo:hi]) + bc_ref[h]
            s_n = _scores(qm, kn_ref[0, :, h * V_DIM + lo:h * V_DIM + hi]) + bn_ref[h]
            mx = jnp.maximum(jnp.max(s_c, axis=-1, keepdims=True), jnp.max(s_n, axis=-1, keepdims=True))
            p_c = jnp.exp(s_c - mx)
            p_n = jnp.exp(s_n - mx)
            l = jnp.sum(p_c, axis=-1, keepdims=True) + jnp.sum(p_n, axis=-1, keepdims=True)
            acc = (jnp.dot(p_c.astype(BF16), vc, preferred_element_type=F32)
                   + jnp.dot(p_n.astype(BF16), vn, preferred_element_type=F32))
            outs.append(acc / l)
        o_ref[0, :, cols] = _diff_out(outs[0], outs[1], lam, g_ref[...], out_scale)


def _sample_attention(scal, q, kc, vc, kn, vn, bias_c, bias_n, subln_g, out_scale):
    bsz, ts, d = q.shape
    past = kc.shape[1]
    per_b = lambda n: pl.BlockSpec((1, n, d), lambda b: (b, 0, 0))
    cache = pl.BlockSpec((1, past, N_HEADS, V_DIM), lambda b: (b, 0, 0, 0))
    return pl.pallas_call(
        functools.partial(_sample_attn_kernel, out_scale=out_scale),
        grid=(bsz,),
        in_specs=[pl.BlockSpec(memory_space=pltpu.SMEM),
                  per_b(ts), cache, cache, per_b(ts), per_b(ts),
                  _const_spec((N_HEADS, ts, past)), _const_spec((N_HEADS, ts, ts)),
                  _const_spec((1, V_DIM))],
        out_specs=per_b(ts),
        out_shape=jax.ShapeDtypeStruct((bsz, ts, d), F32),
        compiler_params=_params("parallel"),
        name="sample_attn",
    )(scal, q, kc, vc, kn, vn, bias_c, bias_n, subln_g)


def _merge_kernel(h_ref, c_ref, a_ref, gs_ref, w_ref, g_ref, b_ref, o_ref, ob_ref, *, d):
    mix = gs_ref[:, 0:d] * c_ref[...] + gs_ref[:, d:2 * d] * a_ref[...]
    y = jnp.dot(mix.astype(BF16), w_ref[...], preferred_element_type=F32)
    h1 = _layer_norm(ALPHA * h_ref[...] + y, g_ref[...], b_ref[...])
    o_ref[...] = h1
    ob_ref[...] = h1.astype(BF16)


def _merge(h, conv_out, attn_out, gs, w_b, ln_g, ln_b, tm):
    r, d = h.shape
    row = lambda n: pl.BlockSpec((tm, n), lambda i: (i, 0))
    return pl.pallas_call(
        functools.partial(_merge_kernel, d=d),
        grid=(r // tm,),
        in_specs=[row(d), row(d), row(d), row(2 * d), _const_spec((d, d)), _const_spec((1, d)),
                  _const_spec((1, d))],
        out_specs=(row(d), row(d)),
        out_shape=(jax.ShapeDtypeStruct((r, d), F32), jax.ShapeDtypeStruct((r, d), BF16)),
        compiler_params=_params("parallel"),
        name="merge_ln1",
    )(h, conv_out, attn_out, gs, w_b, ln_g, ln_b)


def _top16_sorted(s, rowid16):
    work = s
    top = jnp.full(rowid16.shape, -jnp.inf, F32)
    for r in range(PEER_TOPK):
        mx = jnp.max(work, axis=0, keepdims=True)
        top = jnp.where(rowid16 == r, mx, top)
        if r + 1 < PEER_TOPK:
            work = jnp.where(work == mx, -jnp.inf, work)
    return top


def _peer_prep_kernel(x_ref, wq_ref, sk_ref, p_ref, t_ref, q_scr, *, tb):
    q_scr[...] = _scores(wq_ref[...], x_ref[...]).astype(BF16)
    rowid16 = lax.broadcasted_iota(jnp.int32, (PEER_TOPK, tb), 0)
    rowid8 = lax.broadcasted_iota(jnp.int32, (SUBLANES, tb), 0)

    def head(h, _):
        base = pl.multiple_of(h * 2 * PEER_HALF, 2 * PEER_HALF)
        s0 = jnp.dot(sk_ref[0], q_scr[pl.ds(base, PEER_HALF), :], preferred_element_type=F32)
        s1 = jnp.dot(sk_ref[1], q_scr[pl.ds(base + PEER_HALF, PEER_HALF), :], preferred_element_type=F32)
        a = _top16_sorted(s0, rowid16)
        b = _top16_sorted(s1, rowid16)
        cands = [a[0:1] + b[0:8], a[0:1] + b[8:16]]
        for i in range(1, 8):
            cands.append(jnp.where(rowid8 < PEER_TOPK // (i + 1), a[i:i + 1] + b[0:8], -jnp.inf))
        cands.append(a[8:16] + b[0:1])
        top = _top16_sorted(jnp.concatenate(cands, axis=0), rowid16)
        z = jnp.sum(jnp.exp(top - top[0:1]), axis=0, keepdims=True)
        p_ref[h, 0] = s0
        p_ref[h, 1] = jnp.exp(s0 - a[0:1]) / z
        p_ref[h, 2] = s1
        p_ref[h, 3] = jnp.exp(s1 - b[0:1])
        t_ref[h] = jnp.broadcast_to(top[PEER_TOPK - 1:PEER_TOPK], (SUBLANES, tb))
        return 0

    lax.fori_loop(0, PEER_HEADS, head, 0, unroll=4)


def _peer_prep(x, wq_t, sub_keys_b, tb):
    r, d = x.shape
    nq = wq_t.shape[0]
    return pl.pallas_call(
        functools.partial(_peer_prep_kernel, tb=tb),
        grid=(r // tb,),
        in_specs=[pl.BlockSpec((tb, d), lambda i: (i, 0)), _const_spec((nq, d)),
                  _const_spec((2, N_KEYS, PEER_HALF))],
        out_specs=(pl.BlockSpec((PEER_HEADS, 4, N_KEYS, tb), lambda i: (0, 0, 0, i)),
                   pl.BlockSpec((PEER_HEADS, SUBLANES, tb), lambda i: (0, 0, i))),
        out_shape=(jax.ShapeDtypeStruct((PEER_HEADS, 4, N_KEYS, r), F32),
                   jax.ShapeDtypeStruct((PEER_HEADS, SUBLANES, r), F32)),
        scratch_shapes=[pltpu.VMEM((nq, tb), BF16)],
        compiler_params=_params("parallel"),
        name="peer_prep",
    )(x, wq_t, sub_keys_b)


PEER_JT = 32
PEER_SUB = 256
PEER_ZR = 64


def _peer_kernel(unit_ref, x_ref, xb_ref, p_ref, t_ref, u_ref, vt_ref, g_ref, b_ref, o_ref,
                 acc_ref, pre_ref, w_ref, z_ref, vprev_ref, *, ec, tb):
    e = pl.program_id(1)

    @pl.when(e == 0)
    def _():
        acc_ref[...] = jnp.zeros_like(acc_ref)
        z_ref[...] = jnp.zeros_like(z_ref)
        vprev_ref[...] = jnp.zeros_like(vprev_ref)

    n_sub = ec // PEER_SUB
    d = u_ref.shape[1]
    half = tb // 2
    n_k = d // MXU_TILE

    def pre_piece(sc, p):
        n, k = divmod(p, n_k)
        ks = slice(k * MXU_TILE, (k + 1) * MXU_TILE)
        return n, _scores(u_ref[sc * PEER_SUB:(sc + 1) * PEER_SUB, ks], xb_ref[n * half:(n + 1) * half, ks])

    def out_piece(sp, n, v_ref):
        ns = slice(n * half, (n + 1) * half)
        es = slice(sp * 2 * PEER_SUB, (sp + 1) * 2 * PEER_SUB)
        acc_ref[:, ns] += jnp.dot(v_ref[:, es], z_ref[es, ns], preferred_element_type=F32)

    def add_piece(pre, n, piece):
        pre[n] = piece if pre[n] is None else pre[n] + piece

    n_grp = ec // N_KEYS
    i0 = e * n_grp
    s0_rows = [[p_ref[h, 0, pl.ds(i0 + gi, 1), :] for h in range(PEER_HEADS)] for gi in range(n_grp)]
    ea_rows = [[p_ref[h, 1, pl.ds(i0 + gi, 1), :] for h in range(PEER_HEADS)] for gi in range(n_grp)]
    tiles = [(lc, jt) for lc in range(tb // LANES) for jt in range(N_KEYS // PEER_JT)]
    pre_jobs = [(sc, p) for sc in range(n_sub) for p in range(2 * n_k)]
    out_jobs = [(sp, n) for sp in range(ec // (2 * PEER_SUB)) for n in range(2)]
    assert len(pre_jobs) % len(tiles) == 0 and len(tiles) % len(out_jobs) == 0
    pre_per_tile = len(pre_jobs) // len(tiles)
    tiles_per_out = len(tiles) // len(out_jobs)
    pre_acc = [None, None]
    for it, (lc, jt) in enumerate(tiles):
        for sc, p in pre_jobs[it * pre_per_tile:(it + 1) * pre_per_tile]:
            n, piece = pre_piece(sc, p)
            add_piece(pre_acc, n, piece)
            if p % n_k == n_k - 1:
                pre_ref[sc * PEER_SUB:(sc + 1) * PEER_SUB, n * half:(n + 1) * half] = pre_acc[n]
                pre_acc[n] = None
        if it % tiles_per_out == 0:
            sp, n = out_jobs[it // tiles_per_out]
            out_piece(sp, n, vprev_ref)
        lanes = slice(lc * LANES, (lc + 1) * LANES)
        js = slice(jt * PEER_JT, (jt + 1) * PEER_JT)
        w = [jnp.zeros((PEER_JT, LANES), F32) for _ in range(n_grp)]
        for h in range(PEER_HEADS):
            s1 = p_ref[h, 2, js, lanes] * unit_ref[0]
            eb = p_ref[h, 3, js, lanes] * unit_ref[0]
            thr = t_ref[h, 0:1, lanes]
            for gi in range(n_grp):
                sel = jnp.where(s0_rows[gi][h][:, lanes] + s1 >= thr, eb, 0.0)
                w[gi] = w[gi] + sel * ea_rows[gi][h][:, lanes]
        for gi in range(n_grp):
            w_ref[gi * N_KEYS + jt * PEER_JT:gi * N_KEYS + (jt + 1) * PEER_JT, lanes] = w[gi]

    for r0 in range(0, ec, PEER_ZR):
        pre = pre_ref[r0:r0 + PEER_ZR, :]
        act = 0.5 * pre * (1.0 + lax.erf(pre * (2.0 ** -0.5)))
        z_ref[r0:r0 + PEER_ZR, :] = (w_ref[r0:r0 + PEER_ZR, :] * act).astype(BF16)
    vprev_ref[...] = vt_ref[...]

    @pl.when(e == pl.num_programs(1) - 1)
    def _():
        for sp, n in out_jobs:
            out_piece(sp, n, vt_ref)
        y = acc_ref[...].T
        o_ref[...] = _layer_norm(ALPHA * x_ref[...] + y, g_ref[...], b_ref[...])


def _peer(x, xb, pfac, thr, u_b, vt_b, ln_g, ln_b, tb, ec):
    r, d = x.shape
    n_exp = u_b.shape[0]
    return pl.pallas_call(
        functools.partial(_peer_kernel, ec=ec, tb=tb),
        grid=(r // tb, n_exp // ec),
        in_specs=[pl.BlockSpec(memory_space=pltpu.SMEM),
                  pl.BlockSpec((tb, d), lambda i, e: (i, 0)),
                  pl.BlockSpec((tb, d), lambda i, e: (i, 0)),
                  pl.BlockSpec((PEER_HEADS, 4, N_KEYS, tb), lambda i, e: (0, 0, 0, i)),
                  pl.BlockSpec((PEER_HEADS, SUBLANES, tb), lambda i, e: (0, 0, i)),
                  pl.BlockSpec((ec, d), lambda i, e: (e, 0)),
                  pl.BlockSpec((d, ec), lambda i, e: (0, e)),
                  _const_spec((1, d)), _const_spec((1, d))],
        out_specs=pl.BlockSpec((tb, d), lambda i, e: (i, 0)),
        out_shape=jax.ShapeDtypeStruct((r, d), F32),
        scratch_shapes=[pltpu.VMEM((d, tb), F32), pltpu.VMEM((ec, tb), F32), pltpu.VMEM((ec, tb), F32),
                        pltpu.VMEM((ec, tb), BF16), pltpu.VMEM((d, ec), BF16)],
        compiler_params=_params("parallel", "arbitrary"),
        name="peer_experts",
    )(jnp.ones((1,), F32), x, xb, pfac, thr, u_b, vt_b, ln_g, ln_b)


def _t5_bucket(rel):
    half = NUM_BUCKETS // 2
    max_exact = half // 2
    n = jnp.abs(rel)
    large = max_exact + (jnp.log(jnp.maximum(n, 1).astype(F32) / max_exact)
                         / math.log(MAX_DISTANCE / max_exact) * (half - max_exact)).astype(jnp.int32)
    large = jnp.minimum(large, half - 1)
    return jnp.where(rel > 0, half, 0) + jnp.where(n < max_exact, n, large)


def _rel_bias_tile(rel_bias, rel):
    bucket = _t5_bucket(rel)[None]
    tile = jnp.zeros((rel_bias.shape[1],) + rel.shape, F32)
    for b in range(NUM_BUCKETS):
        tile = jnp.where(bucket == b, rel_bias[b].astype(F32)[:, None, None], tile)
    return tile


def _pick(n, prefs):
    for p in prefs:
        if n % p == 0:
            return p
    raise ValueError(f"no supported tile for extent {n}")


def kernel(x_prompt, x_sample, cache_k, cache_v, state_conv, meta_tokens, ln_in_g, ln_in_b, rel_bias, w_in, b_gate, conv_w, conv_b, conv_ln_g, conv_ln_b, w_conv_out, lam_q1, lam_k1, lam_q2, lam_k2, subln_g, w_out, ln1_g, ln1_b, w_peer_q, sub_keys, peer_u, peer_v, ln2_g, ln2_b):
    bsz, t, d = x_prompt.shape
    bd, ts, _ = x_sample.shape
    past = cache_k.shape[2]
    assert w_in.shape[0] == DEPTH == 1
    assert t >= CONV_WIDTH - 1 and ts >= CONV_WIDTH - 1 and ts % CTX_ROWS == 0
    l = 0
    row2 = lambda a: a.reshape(1, -1)

    lam_init = 0.8 - 0.6 * math.exp(-0.3 * l)
    lam = (jnp.exp(jnp.sum(lam_q1[l] * lam_k1[l]).astype(F32))
           - jnp.exp(jnp.sum(lam_q2[l] * lam_k2[l]).astype(F32)) + lam_init)
    scal = jnp.reshape(lam, (1,)).astype(F32)
    out_scale = 1.0 - lam_init

    w_in_b = w_in[l].astype(BF16)
    ln_g, ln_b = row2(ln_in_g), row2(ln_in_b)
    bg = row2(b_gate[l])

    tm = _pick(bsz * t, (256, 128, 64, 32, 16, 8))
    n_s = bd * ts
    tm_s = 256
    pad = (-(n_s + N_META)) % tm_s
    x_small = jnp.concatenate([x_sample.reshape(n_s, d), meta_tokens.astype(F32), jnp.zeros((pad, d), F32)], axis=0)
    hp, up, qp, kp, vp, kpb, vpb, gsp = _inproj(x_prompt.reshape(bsz * t, d), ln_g, ln_b, w_in_b, bg, tm)
    hs, us, qs, ks, vs, ksb, vsb, gss = _inproj(x_small, ln_g, ln_b, w_in_b, bg, tm_s)
    meta = slice(n_s, n_s + N_META)
    as3 = lambda a, b_: a.reshape(b_, -1, a.shape[-1])

    cw = jnp.concatenate([conv_w[l], jnp.zeros((CTX_ROWS - CONV_WIDTH, d), F32)], axis=0)
    conv_args = (cw, row2(conv_b[l]), row2(conv_ln_g[l]), row2(conv_ln_b[l]), w_conv_out[l].astype(BF16))
    ctx_p = jnp.concatenate([jnp.zeros((CTX_ROWS - N_META, d), F32), us[meta]], axis=0)[None]
    ctx_s = jnp.concatenate([jnp.zeros((bd, CTX_PAD, d), F32), state_conv[l]], axis=1)
    up3 = as3(up, bsz)
    us3 = as3(us[:n_s], bd)
    conv_p = _conv_branch(up3, ctx_p, *conv_args, tt=_pick(t, (256, 128, 64, 32)))
    conv_s = _conv_branch(us3, ctx_s, *conv_args, tt=ts)

    tq = _pick(t, (512, 256, 128, 64))
    assert tq % CHUNK == 0 and tq + 1 >= MAX_DISTANCE
    fq = jnp.arange(tq)
    rel_own = fq[None, :] - fq[:, None]
    own = _rel_bias_tile(rel_bias, rel_own) + jnp.where(fq[None, :] // CHUNK <= fq[:, None] // CHUNK, 0.0, NEG_INF)
    prev = _rel_bias_tile(rel_bias, rel_own - tq)
    far = _rel_bias_tile(rel_bias, jnp.full((1, 1), -(tq + 1)))
    bias3 = jnp.stack([prev, own], axis=1)
    scal = jnp.concatenate([scal, far.reshape(N_HEADS)])
    rel_meta = jnp.arange(N_META)[None, :] - (N_META + fq[:, None])
    bm = _rel_bias_tile(rel_bias, rel_meta)
    bm2 = jnp.stack([bm, jnp.broadcast_to(far, bm.shape)], axis=1)
    sg = row2(subln_g[l])
    attn_p = _prompt_attention(scal, as3(qp, bsz), as3(kpb, bsz), as3(vpb, bsz), ksb[meta], vsb[meta],
                               bias3, bm2, sg, tq, out_scale)

    k_pos = jnp.arange(past + ts)
    rel_s = k_pos[None, :] - k_pos[past:, None]
    bias_s = _rel_bias_tile(rel_bias, rel_s)
    attn_s = _sample_attention(scal, as3(qs[:n_s], bd), cache_k[l], cache_v[l], as3(ksb[:n_s], bd), as3(vsb[:n_s], bd),
                               bias_s[:, :, :past], bias_s[:, :, past:], sg, out_scale)

    w_out_b = w_out[l].astype(BF16)
    h1p = _merge(hp, conv_p.reshape(-1, d), attn_p.reshape(-1, d), gsp, w_out_b, row2(ln1_g[l]), row2(ln1_b[l]), tm)
    h1s = _merge(hs[:n_s], conv_s.reshape(-1, d), attn_s.reshape(-1, d), gss[:n_s], w_out_b,
                 row2(ln1_g[l]), row2(ln1_b[l]), _pick(n_s, (256, 128, 64, 32, 16, 8)))

    wq_t = w_peer_q[l].T.astype(BF16)
    sk_b = sub_keys[l].astype(BF16)
    u_b = peer_u[l].astype(BF16)
    vt_b = peer_v[l].T.astype(BF16)

    def peer_ln2(h1, h1b):
        tb = _pick(h1.shape[0], (512, 256, 128))
        pfac, thr = _peer_prep(h1b, wq_t, sk_b, _pick(h1.shape[0], (256, 128)))
        return _peer(h1, h1b, pfac, thr, u_b, vt_b, row2(ln2_g[l]), row2(ln2_b[l]), tb, ec=1024)

    y_p = peer_ln2(*h1p).reshape(bsz, t, d)
    y_s = peer_ln2(*h1s).reshape(bd, ts, d)

    heads = lambda a, b_: a.reshape(1, b_, -1, N_HEADS, V_DIM)
    with_meta = lambda rows, m: jnp.concatenate([jnp.broadcast_to(m[None], (bsz, N_META, d)), as3(rows, bsz)], axis=1)
    keep = CONV_WIDTH - 1
    return (y_p, y_s,
            heads(with_meta(kp, ks[meta]), bsz), heads(with_meta(vp, vs[meta]), bsz), up3[None, :, t - keep:],
            heads(ks[:n_s], bd), heads(vs[:n_s], bd), us3[None, :, ts - keep:])
```
